```python
import math
import jax, jax.numpy as jnp
from jax import lax
import numpy as np

D_MODEL = 2048
BATCH = 4
SEQ = 2048
DEPTH = 4

N_MIXERS = 4
N_LAYERS_A = (DEPTH + N_MIXERS - 1) // N_MIXERS
N_LAYERS_B = (DEPTH + N_MIXERS - 2) // N_MIXERS
N_LAYERS_C = (DEPTH + N_MIXERS - 3) // N_MIXERS
N_LAYERS_D = (DEPTH + N_MIXERS - 4) // N_MIXERS

MEM_LEN = 256
D_FF = 5632
RMS_EPS = 1e-6
BLOCK = 128

MEM_HEADS = 4
MEM_HEAD_DIM = 128
MEM_WIDTH = MEM_HEADS * MEM_HEAD_DIM

CONV_WIDTH = D_MODEL
CONV_K = 3
CONV_IN = 3 * CONV_WIDTH

MLA_HEADS = 16
MLA_Q_RANK = 512
MLA_KV_RANK = 512
MLA_NOPE = 128
MLA_ROPE = 64
MLA_V = 128
MLA_QK = MLA_NOPE + MLA_ROPE
MLA_IN = MLA_Q_RANK + MLA_KV_RANK + MLA_ROPE
ROPE_THETA = 10000.0

SWA_Q_HEADS = 32
SWA_KV_HEADS = 4
SWA_HEAD_DIM = 64
WINDOW = 128
SWA_IN = (SWA_Q_HEADS + 2 * SWA_KV_HEADS) * SWA_HEAD_DIM
REL_BUCKETS = 32
REL_MAX_DIST = 128

FOX_HEADS = 32
FOX_HEAD_DIM = 64
FOX_WIDTH = FOX_HEADS * FOX_HEAD_DIM
FOX_IN = 3 * FOX_WIDTH + FOX_HEADS

kernel_name = "hybrid_interleaved_macaron_trunk"


def rms_norm(x, g):
    xf = x.astype(jnp.float32)
    y = xf * lax.rsqrt(jnp.mean(xf * xf, axis=-1, keepdims=True) + RMS_EPS)
    return (y * g.astype(jnp.float32)).astype(x.dtype)


def swiglu(h, w_gate, w_up, w_down):
    return (jax.nn.silu(h @ w_gate) * (h @ w_up)) @ w_down


def rope(t, pos):
    half = t.shape[-1] // 2
    inv = ROPE_THETA ** (-jnp.arange(half, dtype=jnp.float32) / half)
    ang = pos.astype(jnp.float32)[:, None] * inv
    cos = jnp.cos(ang)[:, None, :]
    sin = jnp.sin(ang)[:, None, :]
    t1 = t[..., :half].astype(jnp.float32)
    t2 = t[..., half:].astype(jnp.float32)
    return jnp.concatenate([t1 * cos - t2 * sin, t1 * sin + t2 * cos], axis=-1).astype(t.dtype)


def t5_causal_bucket(dist):
    exact = REL_BUCKETS // 2
    d = np.maximum(dist, 0)
    log_b = exact + (np.log(np.maximum(d, 1) / exact) / np.log(REL_MAX_DIST / exact)
                     * (REL_BUCKETS - exact)).astype(np.int32)
    log_b = np.minimum(log_b, REL_BUCKETS - 1)
    return np.where(d < exact, d, log_b).astype(np.int32)


def causal_attention_blocks(q, k, v, scale, log_f_cum=None):
    b, s, h, _ = q.shape
    nblk = s // BLOCK
    k_pos = jnp.arange(s)
    c_bhs = None if log_f_cum is None else log_f_cum.astype(jnp.float32).transpose(0, 2, 1)

    def one_block(i):
        start = i * BLOCK
        qb = lax.dynamic_slice_in_dim(q, start, BLOCK, axis=1)
        logits = jnp.einsum('bqhd,bkhd->bhqk', qb, k, preferred_element_type=jnp.float32) * scale
        if c_bhs is not None:
            cq = lax.dynamic_slice_in_dim(c_bhs, start, BLOCK, axis=2)
            logits = logits + (cq[..., :, None] - c_bhs[..., None, :])
        q_pos = start + jnp.arange(BLOCK)
        mask = k_pos[None, :] <= q_pos[:, None]
        logits = jnp.where(mask, logits, -jnp.inf)
        p = jax.nn.softmax(logits, axis=-1).astype(v.dtype)
        return jnp.einsum('bhqk,bkhd->bqhd', p, v)

    out = lax.map(one_block, jnp.arange(nblk))
    return out.swapaxes(0, 1).reshape(b, s, h, v.shape[-1])


def sliding_window_attention(q, k, v, sinks, rel_bias):
    b, s, hq, d = q.shape
    hkv = k.shape[2]
    g = hq // hkv
    nblk = s // BLOCK
    qb = q.reshape(b, nblk, BLOCK, hkv, g, d)

    def band(t):
        tb = t.reshape(b, nblk, BLOCK, hkv, d)
        prev = jnp.pad(tb, ((0, 0), (1, 0), (0, 0), (0, 0), (0, 0)))[:, :-1]
        return jnp.concatenate([prev, tb], axis=2)

    kb, vb = band(k), band(v)
    logits = jnp.einsum('bnqhgd,bnkhd->bnhgqk', qb, kb,
                        preferred_element_type=jnp.float32) * (1.0 / math.sqrt(d))
    dist = np.arange(BLOCK)[:, None] + BLOCK - np.arange(2 * BLOCK)[None, :]
    in_window = (dist >= 0) & (dist < WINDOW)
    bias = rel_bias.astype(jnp.float32)[t5_causal_bucket(dist)]
    bias = bias.reshape(BLOCK, 2 * BLOCK, hkv, g).transpose(2, 3, 0, 1)
    key_valid = (np.arange(nblk)[:, None] * BLOCK - BLOCK + np.arange(2 * BLOCK)[None, :]) >= 0
    mask = in_window[None] & key_valid[:, None, :]
    logits = jnp.where(mask[None, :, None, None], logits + bias, -jnp.inf)
    sink = sinks.astype(jnp.float32).reshape(hkv, g)[None, None, :, :, None, None]
    sink = jnp.broadcast_to(sink, logits.shape[:-1] + (1,))
    probs = jax.nn.softmax(jnp.concatenate([logits, sink], axis=-1), axis=-1)[..., :-1]
    out = jnp.einsum('bnhgqk,bnkhd->bnqhgd', probs.astype(v.dtype), vb)
    return out.reshape(b, s, hq, d)


def memory_attention(mq, mk, mv):
    logits = jnp.einsum('bshd,bmhd->bhsm', mq, mk,
                        preferred_element_type=jnp.float32) * (1.0 / math.sqrt(MEM_HEAD_DIM))
    p = jax.nn.softmax(logits, axis=-1).astype(mv.dtype)
    return jnp.einsum('bhsm,bmhd->bshd', p, mv)


def short_conv_mixer(u, conv_w):
    s = u.shape[1]
    gb, gc, xt = jnp.split(u, 3, axis=-1)
    z = gc * xt
    zp = jnp.pad(z, ((0, 0), (CONV_K - 1, 0), (0, 0)))
    conv = zp[:, 0:s] * conv_w[0]
    for tap in range(1, CONV_K):
        conv = conv + zp[:, tap:tap + s] * conv_w[tap]
    return gb * conv


def mla_mixer(u, q_a_norm, w_q_b, kv_a_norm, w_kv_b, q_norm, k_norm):
    b, s, _ = u.shape
    q_lat = u[..., :MLA_Q_RANK]
    kv_lat = u[..., MLA_Q_RANK:MLA_Q_RANK + MLA_KV_RANK]
    k_rope = u[..., MLA_Q_RANK + MLA_KV_RANK:].reshape(b, s, 1, MLA_ROPE)
    q = (rms_norm(q_lat, q_a_norm) @ w_q_b).reshape(b, s, MLA_HEADS, MLA_QK)
    kv = (rms_norm(kv_lat, kv_a_norm) @ w_kv_b).reshape(b, s, MLA_HEADS, MLA_NOPE + MLA_V)
    k_nope, v = kv[..., :MLA_NOPE], kv[..., MLA_NOPE:]
    pos = jnp.arange(s)
    q_nope = rms_norm(q[..., :MLA_NOPE], q_norm[:MLA_NOPE])
    q_rot = rope(rms_norm(q[..., MLA_NOPE:], q_norm[MLA_NOPE:]), pos)
    k_nope = rms_norm(k_nope, k_norm[:MLA_NOPE])
    k_rot = rope(rms_norm(k_rope, k_norm[MLA_NOPE:]), pos)
    k_rot = jnp.broadcast_to(k_rot, (b, s, MLA_HEADS, MLA_ROPE))
    q_full = jnp.concatenate([q_nope, q_rot], axis=-1)
    k_full = jnp.concatenate([k_nope, k_rot], axis=-1)
    out = causal_attention_blocks(q_full, k_full, v, 1.0 / math.sqrt(MLA_QK))
    return out.reshape(b, s, MLA_HEADS * MLA_V)


def swa_mixer(u, q_norm, k_norm, sinks, rel_bias):
    b, s, _ = u.shape
    nq = SWA_Q_HEADS * SWA_HEAD_DIM
    nk = SWA_KV_HEADS * SWA_HEAD_DIM
    q = rms_norm(u[..., :nq].reshape(b, s, SWA_Q_HEADS, SWA_HEAD_DIM), q_norm)
    k = rms_norm(u[..., nq:nq + nk].reshape(b, s, SWA_KV_HEADS, SWA_HEAD_DIM), k_norm)
    v = u[..., nq + nk:].reshape(b, s, SWA_KV_HEADS, SWA_HEAD_DIM)
    out = sliding_window_attention(q, k, v, sinks, rel_bias)
    return out.reshape(b, s, nq)


def fox_mixer(u, b_f, q_norm, k_norm):
    b, s, _ = u.shape
    q = rms_norm(u[..., :FOX_WIDTH].reshape(b, s, FOX_HEADS, FOX_HEAD_DIM), q_norm)
    k = rms_norm(u[..., FOX_WIDTH:2 * FOX_WIDTH].reshape(b, s, FOX_HEADS, FOX_HEAD_DIM), k_norm)
    v = u[..., 2 * FOX_WIDTH:3 * FOX_WIDTH].reshape(b, s, FOX_HEADS, FOX_HEAD_DIM)
    f_logit = u[..., 3 * FOX_WIDTH:].astype(jnp.float32) + b_f.astype(jnp.float32)
    log_f_cum = jnp.cumsum(jax.nn.log_sigmoid(f_logit), axis=1)
    out = causal_attention_blocks(q, k, v, 1.0 / math.sqrt(FOX_HEAD_DIM), log_f_cum)
    return out.reshape(b, s, FOX_WIDTH)


def setup_inputs(seed: int = 0) -> dict:
    key = jax.random.key(seed)
    keys = iter(jax.random.split(key, 64))

    def nrm(shape, fan_in):
        return jax.random.normal(next(keys), shape, jnp.float32) * (fan_in ** -0.5)

    def gain(shape):
        return 1.0 + 0.1 * jax.random.normal(next(keys), shape, jnp.float32)

    d, f = D_MODEL, D_FF
    inp = {}
    inp["x"] = jax.random.normal(next(keys), (BATCH, SEQ, d), jnp.float32)
    inp["mem"] = jax.random.normal(next(keys), (BATCH, MEM_LEN, d), jnp.float32)
    inp["norm_ffn1"] = gain((DEPTH, d))
    inp["ffn1_w_gate"] = nrm((DEPTH, d, f), d)
    inp["ffn1_w_up"] = nrm((DEPTH, d, f), d)
    inp["ffn1_w_down"] = nrm((DEPTH, f, d), f)
    inp["norm_mix"] = gain((DEPTH, d))
    inp["norm_ffn2"] = gain((DEPTH, d))
    inp["ffn2_w_gate"] = nrm((DEPTH, d, f), d)
    inp["ffn2_w_up"] = nrm((DEPTH, d, f), d)
    inp["ffn2_w_down"] = nrm((DEPTH, f, d), f)
    inp["norm_mem"] = gain((DEPTH, d))
    inp["mem_w_kv"] = nrm((DEPTH, d, 2 * MEM_WIDTH), d)
    inp["mem_q_norm"] = gain((DEPTH, MEM_HEAD_DIM))
    inp["mem_k_norm"] = gain((DEPTH, MEM_HEAD_DIM))
    inp["conv_w_in"] = nrm((N_LAYERS_A, d, CONV_IN + MEM_WIDTH), d)
    inp["conv_w"] = nrm((N_LAYERS_A, CONV_K, CONV_WIDTH), CONV_K)
    inp["conv_w_out"] = nrm((N_LAYERS_A, CONV_WIDTH + MEM_WIDTH, d), CONV_WIDTH + MEM_WIDTH)
    inp["mla_w_in"] = nrm((N_LAYERS_B, d, MLA_IN + MEM_WIDTH), d)
    inp["mla_q_a_norm"] = gain((N_LAYERS_B, MLA_Q_RANK))
    inp["mla_w_q_b"] = nrm((N_LAYERS_B, MLA_Q_RANK, MLA_HEADS * MLA_QK), MLA_Q_RANK)
    inp["mla_kv_a_norm"] = gain((N_LAYERS_B, MLA_KV_RANK))
    inp["mla_w_kv_b"] = nrm((N_LAYERS_B, MLA_KV_RANK, MLA_HEADS * (MLA_NOPE + MLA_V)), MLA_KV_RANK)
    inp["mla_q_norm"] = gain((N_LAYERS_B, MLA_QK))
    inp["mla_k_norm"] = gain((N_LAYERS_B, MLA_QK))
    inp["mla_w_out"] = nrm((N_LAYERS_B, MLA_HEADS * MLA_V + MEM_WIDTH, d), MLA_HEADS * MLA_V + MEM_WIDTH)
    inp["swa_w_in"] = nrm((N_LAYERS_C, d, SWA_IN + MEM_WIDTH), d)
    inp["swa_q_norm"] = gain((N_LAYERS_C, SWA_HEAD_DIM))
    inp["swa_k_norm"] = gain((N_LAYERS_C, SWA_HEAD_DIM))
    inp["swa_sinks"] = 0.5 * jax.random.normal(next(keys), (N_LAYERS_C, SWA_Q_HEADS), jnp.float32)
    swa_out_in = SWA_Q_HEADS * SWA_HEAD_DIM + MEM_WIDTH
    inp["swa_w_out"] = nrm((N_LAYERS_C, swa_out_in, d), swa_out_in)
    inp["rel_bias"] = 0.5 * jax.random.normal(next(keys), (REL_BUCKETS, SWA_Q_HEADS), jnp.float32)
    inp["fox_w_in"] = nrm((N_LAYERS_D, d, FOX_IN + MEM_WIDTH), d)
    inp["fox_b_f"] = jax.random.uniform(next(keys), (N_LAYERS_D, FOX_HEADS), jnp.float32, 1.0, 4.0)
    inp["fox_q_norm"] = gain((N_LAYERS_D, FOX_HEAD_DIM))
    inp["fox_k_norm"] = gain((N_LAYERS_D, FOX_HEAD_DIM))
    inp["fox_w_out"] = nrm((N_LAYERS_D, FOX_WIDTH + MEM_WIDTH, d), FOX_WIDTH + MEM_WIDTH)
    return inp


def reference(x, mem, norm_ffn1, ffn1_w_gate, ffn1_w_up, ffn1_w_down, norm_mix,
              norm_ffn2, ffn2_w_gate, ffn2_w_up, ffn2_w_down,
              norm_mem, mem_w_kv, mem_q_norm, mem_k_norm,
              conv_w_in, conv_w, conv_w_out,
              mla_w_in, mla_q_a_norm, mla_w_q_b, mla_kv_a_norm, mla_w_kv_b, mla_q_norm, mla_k_norm, mla_w_out,
              swa_w_in, swa_q_norm, swa_k_norm, swa_sinks, swa_w_out, rel_bias,
              fox_w_in, fox_b_f, fox_q_norm, fox_k_norm, fox_w_out):
    b, s, _ = x.shape
    m_len = mem.shape[1]
    for i in range(DEPTH):
        kind, occ = i % N_MIXERS, i // N_MIXERS
        x = x + 0.5 * swiglu(rms_norm(x, norm_ffn1[i]), ffn1_w_gate[i], ffn1_w_up[i], ffn1_w_down[i])
        h = rms_norm(x, norm_mix[i])
        if kind == 0:
            u = h @ conv_w_in[occ]
            mix = short_conv_mixer(u[..., :-MEM_WIDTH], conv_w[occ])
            w_out = conv_w_out[occ]
        elif kind == 1:
            u = h @ mla_w_in[occ]
            mix = mla_mixer(u[..., :-MEM_WIDTH], mla_q_a_norm[occ], mla_w_q_b[occ], mla_kv_a_norm[occ],
                            mla_w_kv_b[occ], mla_q_norm[occ], mla_k_norm[occ])
            w_out = mla_w_out[occ]
        elif kind == 2:
            u = h @ swa_w_in[occ]
            mix = swa_mixer(u[..., :-MEM_WIDTH], swa_q_norm[occ], swa_k_norm[occ], swa_sinks[occ], rel_bias)
            w_out = swa_w_out[occ]
        else:
            u = h @ fox_w_in[occ]
            mix = fox_mixer(u[..., :-MEM_WIDTH], fox_b_f[occ], fox_q_norm[occ], fox_k_norm[occ])
            w_out = fox_w_out[occ]
        mkv = (rms_norm(mem, norm_mem[i]) @ mem_w_kv[i]).reshape(b, m_len, 2, MEM_HEADS, MEM_HEAD_DIM)
        mk = rms_norm(mkv[:, :, 0], mem_k_norm[i])
        mv = mkv[:, :, 1]
        mq = rms_norm(u[..., -MEM_WIDTH:].reshape(b, s, MEM_HEADS, MEM_HEAD_DIM), mem_q_norm[i])
        mem_out = memory_attention(mq, mk, mv).reshape(b, s, MEM_WIDTH)
        x = x + jnp.concatenate([mix, mem_out], axis=-1) @ w_out
        x = x + 0.5 * swiglu(rms_norm(x, norm_ffn2[i]), ffn2_w_gate[i], ffn2_w_up[i], ffn2_w_down[i])
    return x
```

```python
import functools
import math

import numpy as np
import jax
import jax.numpy as jnp
from jax import lax
from jax.experimental import pallas as pl
from jax.experimental.pallas import tpu as pltpu

F32 = jnp.float32
BF16 = jnp.bfloat16

RMS_EPS = 1e-6
NEG_BIG = -1e30
LANES = 128
HALF_LANES = LANES // 2
VMEM_LIMIT_BYTES = 56 * 1024 * 1024

BLOCK = 128
MEM_HEADS = 4
MEM_HEAD_DIM = 128
MEM_WIDTH = MEM_HEADS * MEM_HEAD_DIM
CONV_K = 3
MLA_HEADS = 16
MLA_Q_RANK = 512
MLA_KV_RANK = 512
MLA_NOPE = 128
MLA_ROPE = 64
MLA_V = 128
MLA_QK = MLA_NOPE + MLA_ROPE
MLA_QPAD = 2 * LANES
ROPE_THETA = 10000.0
SWA_Q_HEADS = 32
SWA_KV_HEADS = 4
SWA_GROUP = SWA_Q_HEADS // SWA_KV_HEADS
SWA_HEAD_DIM = 64
WINDOW = 128
REL_BUCKETS = 32
REL_MAX_DIST = 128
FOX_HEADS = 32
FOX_HEAD_DIM = 64
FOX_WIDTH = FOX_HEADS * FOX_HEAD_DIM


def _params(*semantics):
    return pltpu.CompilerParams(dimension_semantics=semantics, vmem_limit_bytes=VMEM_LIMIT_BYTES)


def _pick_tile(n, target):
    best = None
    for t in range(LANES, min(n, target) + 1, LANES):
        if n % t == 0:
            best = t
    return n if best is None else best


def _rms_rows(x, gain):
    r = lax.rsqrt(jnp.mean(x * x, axis=-1, keepdims=True) + RMS_EPS)
    return x * r * gain


def _rms_half_lanes(x, gain):
    lane = lax.broadcasted_iota(jnp.int32, x.shape, 1)
    low = lane < HALF_LANES
    sq = x * x
    s_low = jnp.sum(jnp.where(low, sq, 0.0), axis=-1, keepdims=True)
    s_high = jnp.sum(jnp.where(low, 0.0, sq), axis=-1, keepdims=True)
    r = lax.rsqrt(jnp.where(low, s_low, s_high) * (1.0 / HALF_LANES) + RMS_EPS)
    return x * r * gain


def _dot(a, b):
    return jnp.dot(a, b, preferred_element_type=F32)


def _dot_nt(a, b):
    return lax.dot_general(a, b, (((1,), (1,)), ((), ())), preferred_element_type=F32)


def _ffn_kernel(x_ref, g_ref, wg_ref, wu_ref, wd_ref, o_ref, h_scr, acc_scr):
    f = pl.program_id(1)

    @pl.when(f == 0)
    def _():
        h_scr[...] = _rms_rows(x_ref[...], g_ref[...]).astype(BF16)

    h = h_scr[...]
    gate = _dot(h, wg_ref[...])
    up = _dot(h, wu_ref[...])
    act = (gate / (1.0 + jnp.exp(-gate)) * up).astype(BF16)
    part = _dot(act, wd_ref[...])

    @pl.when(f == 0)
    def _():
        acc_scr[...] = part

    @pl.when(f > 0)
    def _():
        acc_scr[...] += part

    @pl.when(f == pl.num_programs(1) - 1)
    def _():
        o_ref[...] = x_ref[...] + 0.5 * acc_scr[...]


def _ffn(x, gain, w_gate, w_up, w_down):
    t, d = x.shape
    f = w_gate.shape[1]
    tm = _pick_tile(t, 512)
    tf = _pick_tile(f, 512)
    return pl.pallas_call(
        _ffn_kernel,
        grid=(t // tm, f // tf),
        in_specs=[
            pl.BlockSpec((tm, d), lambda i, j: (i, 0)),
            pl.BlockSpec((1, d), lambda i, j: (0, 0)),
            pl.BlockSpec((d, tf), lambda i, j: (0, j)),
            pl.BlockSpec((d, tf), lambda i, j: (0, j)),
            pl.BlockSpec((tf, d), lambda i, j: (j, 0)),
        ],
        out_specs=pl.BlockSpec((tm, d), lambda i, j: (i, 0)),
        out_shape=jax.ShapeDtypeStruct((t, d), F32),
        scratch_shapes=[pltpu.VMEM((tm, d), BF16), pltpu.VMEM((tm, d), F32)],
        compiler_params=_params("parallel", "arbitrary"),
        name="ffn",
    )(x, gain.reshape(1, d), w_gate, w_up, w_down)


def _norm_matmul_kernel(x_ref, g_ref, w_ref, o_ref, h_scr):
    @pl.when(pl.program_id(1) == 0)
    def _():
        h_scr[...] = _rms_rows(x_ref[...].astype(F32), g_ref[...]).astype(BF16)

    o_ref[...] = _dot(h_scr[...], w_ref[...]).astype(o_ref.dtype)


def _norm_matmul(x, gain, w, *, col_block=0, out_dtype=BF16, name="norm_matmul"):
    m = x.shape[0]
    k, n = w.shape
    tm = _pick_tile(m, 512)
    tn = n if n <= 2048 else _pick_tile(n, 1024)
    return pl.pallas_call(
        _norm_matmul_kernel,
        grid=(m // tm, n // tn),
        in_specs=[
            pl.BlockSpec((tm, k), lambda i, j: (i, col_block)),
            pl.BlockSpec((1, k), lambda i, j: (0, 0)),
            pl.BlockSpec((k, tn), lambda i, j: (0, j)),
        ],
        out_specs=pl.BlockSpec((tm, tn), lambda i, j: (i, j)),
        out_shape=jax.ShapeDtypeStruct((m, n), out_dtype),
        scratch_shapes=[pltpu.VMEM((tm, k), BF16)],
        compiler_params=_params("parallel", "arbitrary"),
        name=name,
    )(x, gain.reshape(1, k), w)


def _out_proj_kernel(a_ref, b_ref, wa_ref, wb_ref, x_ref, o_ref):
    acc = _dot(a_ref[...], wa_ref[...])
    acc = acc + _dot(b_ref[...], wb_ref[...])
    o_ref[...] = x_ref[...] + acc


def _out_proj(mix, mem_out, w_mix, w_mem, x):
    t, d = x.shape
    ka = mix.shape[1]
    kb = mem_out.shape[1]
    tm = _pick_tile(t, 512)
    tn = _pick_tile(d, 1024)
    return pl.pallas_call(
        _out_proj_kernel,
        grid=(t // tm, d // tn),
        in_specs=[
            pl.BlockSpec((tm, ka), lambda i, j: (i, 0)),
            pl.BlockSpec((tm, kb), lambda i, j: (i, 0)),
            pl.BlockSpec((ka, tn), lambda i, j: (0, j)),
            pl.BlockSpec((kb, tn), lambda i, j: (0, j)),
            pl.BlockSpec((tm, tn), lambda i, j: (i, j)),
        ],
        out_specs=pl.BlockSpec((tm, tn), lambda i, j: (i, j)),
        out_shape=jax.ShapeDtypeStruct((t, d), F32),
        compiler_params=_params("parallel", "parallel"),
        name="out_proj",
    )(mix, mem_out, w_mix, w_mem, x)


def _mem_attn_kernel(q_ref, kv_ref, gq_ref, gk_ref, o_ref):
    scale = 1.0 / math.sqrt(MEM_HEAD_DIM)
    for h in range(MEM_HEADS):
        cols = slice(h * MEM_HEAD_DIM, (h + 1) * MEM_HEAD_DIM)
        vcols = slice(MEM_WIDTH + h * MEM_HEAD_DIM, MEM_WIDTH + (h + 1) * MEM_HEAD_DIM)
        q = (_rms_rows(q_ref[:, cols].astype(F32), gq_ref[...]) * scale).astype(BF16)
        k = _rms_rows(kv_ref[:, cols].astype(F32), gk_ref[...]).astype(BF16)
        s = _dot_nt(q, k)
        m = jnp.max(s, axis=-1, keepdims=True)
        p = jnp.exp(s - m)
        l = jnp.sum(p, axis=-1, keepdims=True)
        o = _dot(p.astype(BF16), kv_ref[:, vcols]) / l
        o_ref[:, cols] = o.astype(o_ref.dtype)


def _mem_attn(u, q_col_block, mkv, gq, gk, batch):
    t = u.shape[0]
    s = t // batch
    m_len = mkv.shape[0] // batch
    tq = _pick_tile(s, 512)
    nq = s // tq
    return pl.pallas_call(
        _mem_attn_kernel,
        grid=(batch, nq),
        in_specs=[
            pl.BlockSpec((tq, MEM_WIDTH), lambda b, i: (b * nq + i, q_col_block)),
            pl.BlockSpec((m_len, 2 * MEM_WIDTH), lambda b, i: (b, 0)),
            pl.BlockSpec((1, MEM_HEAD_DIM), lambda b, i: (0, 0)),
            pl.BlockSpec((1, MEM_HEAD_DIM), lambda b, i: (0, 0)),
        ],
        out_specs=pl.BlockSpec((tq, MEM_WIDTH), lambda b, i: (b * nq + i, 0)),
        out_shape=jax.ShapeDtypeStruct((t, MEM_WIDTH), BF16),
        compiler_params=_params("parallel", "parallel"),
        name="mem_attn",
    )(u, mkv, gq.reshape(1, MEM_HEAD_DIM), gk.reshape(1, MEM_HEAD_DIM))


CONV_HALO = 16


def _conv_kernel(gb_ref, gc_ref, xt_ref, gch_ref, xth_ref, w_ref, o_ref, *, tiles_per_seq):
    i = pl.program_id(0)
    z = gc_ref[...].astype(F32) * xt_ref[...].astype(F32)
    zh = gch_ref[...].astype(F32) * xth_ref[...].astype(F32)
    zh = jnp.where(i % tiles_per_seq == 0, 0.0, zh)
    zc = jnp.concatenate([zh[CONV_HALO - 8:], z], axis=0)
    z1 = pltpu.roll(zc, 1, 0)[8:]
    z2 = pltpu.roll(zc, 2, 0)[8:]
    w = w_ref[...]
    conv = z2 * w[0:1] + z1 * w[1:2] + z * w[2:3]
    o_ref[...] = (gb_ref[...].astype(F32) * conv).astype(o_ref.dtype)


def _conv_mixer(u, conv_w, batch):
    t = u.shape[0]
    s = t // batch
    c = conv_w.shape[1]
    ts = _pick_tile(s, 512)
    tc = _pick_tile(c, 512)
    nc = c // tc
    halo_per_tile = ts // CONV_HALO

    def halo_map(col0):
        return lambda i, j: (jnp.maximum(i * halo_per_tile - 1, 0), col0 + j)

    return pl.pallas_call(
        functools.partial(_conv_kernel, tiles_per_seq=s // ts),
        grid=(t // ts, nc),
        in_specs=[
            pl.BlockSpec((ts, tc), lambda i, j: (i, j)),
            pl.BlockSpec((ts, tc), lambda i, j: (i, nc + j)),
            pl.BlockSpec((ts, tc), lambda i, j: (i, 2 * nc + j)),
            pl.BlockSpec((CONV_HALO, tc), halo_map(nc)),
            pl.BlockSpec((CONV_HALO, tc), halo_map(2 * nc)),
            pl.BlockSpec((CONV_K, tc), lambda i, j: (0, j)),
        ],
        out_specs=pl.BlockSpec((ts, tc), lambda i, j: (i, j)),
        out_shape=jax.ShapeDtypeStruct((t, c), BF16),
        compiler_params=_params("parallel", "parallel"),
        name="conv_mixer",
    )(u, u, u, u, u, conv_w)


def _online_softmax_step(s, v, m_ref, l_ref, acc_ref):
    m_prev = m_ref[...]
    m_new = jnp.maximum(m_prev, jnp.max(s, axis=-1, keepdims=True))
    alpha = jnp.exp(m_prev - m_new)
    p = jnp.exp(s - m_new)
    l_ref[...] = alpha * l_ref[...] + jnp.sum(p, axis=-1, keepdims=True)
    acc_ref[...] = alpha * acc_ref[...] + _dot(p.astype(BF16), v)
    m_ref[...] = m_new


def _causal_mask(s):
    row = lax.broadcasted_iota(jnp.int32, s.shape, 0)
    col = lax.broadcasted_iota(jnp.int32, s.shape, 1)
    return jnp.where(col <= row, s, NEG_BIG)


def _swap_rope_halves(t):
    lane = lax.broadcasted_iota(jnp.int32, t.shape, 1)
    up = pltpu.roll(t, MLA_ROPE // 2, 1)
    down = pltpu.roll(t, LANES - MLA_ROPE // 2, 1)
    return jnp.where(lane < MLA_ROPE // 2, down, up)


def _rope_lanes(t, cos_t, sin_t):
    return t * cos_t + _swap_rope_halves(t) * sin_t


def _rms_rope_part(x, gain):
    r = lax.rsqrt(jnp.sum(x * x, axis=-1, keepdims=True) * (1.0 / MLA_ROPE) + RMS_EPS)
    return x * r * gain


def _mla_attn_kernel(q_ref, kn_ref, v_ref, kr_ref, cosq_ref, sinq_ref, cosk_ref, sink_ref,
                     gq_ref, gk_ref, o_ref, k_scr, m_scr, l_scr, acc_scr, *, tq):
    i = pl.program_id(2)

    @pl.when(i == 0)
    def _():
        k_scr[:, :LANES] = _rms_rows(kn_ref[...].astype(F32), gk_ref[:, :LANES]).astype(BF16)
        kr = _rms_rope_part(kr_ref[...].astype(F32), gk_ref[:, LANES:])
        k_scr[:, LANES:] = _rope_lanes(kr, cosk_ref[...], sink_ref[...]).astype(BF16)

    scale = 1.0 / math.sqrt(MLA_QK)
    q_nope = _rms_rows(q_ref[:, :LANES].astype(F32), gq_ref[:, :LANES])
    q_rot = _rms_rope_part(q_ref[:, LANES:].astype(F32), gq_ref[:, LANES:])
    q_rot = _rope_lanes(q_rot, cosq_ref[...], sinq_ref[...])
    q = (jnp.concatenate([q_nope, q_rot], axis=-1) * scale).astype(BF16)

    m_scr[...] = jnp.full(m_scr.shape, NEG_BIG, F32)
    l_scr[...] = jnp.zeros(l_scr.shape, F32)
    acc_scr[...] = jnp.zeros(acc_scr.shape, F32)

    def chunk(j, masked):
        rows = pl.ds(pl.multiple_of(j * tq, tq), tq)
        s = _dot_nt(q, k_scr[rows, :])
        if masked:
            s = _causal_mask(s)
        _online_softmax_step(s, v_ref[rows, :], m_scr, l_scr, acc_scr)

    def body(j, carry):
        chunk(j, False)
        return carry

    lax.fori_loop(0, i, body, 0)
    chunk(i, True)
    o_ref[...] = (acc_scr[...] / l_scr[...]).astype(o_ref.dtype)


def _mla_attention(q, kv, u, krope_block, cos_t, sin_t, gq, gk, batch):
    t = q.shape[0]
    s = t // batch
    tq = 512 if s % 512 == 0 and s > 512 else BLOCK
    nq = s // tq
    h = MLA_HEADS
    kernel = functools.partial(_mla_attn_kernel, tq=tq)
    return pl.pallas_call(
        kernel,
        grid=(batch, h, nq),
        in_specs=[
            pl.BlockSpec((tq, MLA_QPAD), lambda b, hh, i: (b * nq + i, hh)),
            pl.BlockSpec((s, LANES), lambda b, hh, i: (b, hh)),
            pl.BlockSpec((s, LANES), lambda b, hh, i: (b, h + hh)),
            pl.BlockSpec((s, LANES), lambda b, hh, i: (b, krope_block)),
            pl.BlockSpec((tq, LANES), lambda b, hh, i: (i, 0)),
            pl.BlockSpec((tq, LANES), lambda b, hh, i: (i, 0)),
            pl.BlockSpec((s, LANES), lambda b, hh, i: (0, 0)),
            pl.BlockSpec((s, LANES), lambda b, hh, i: (0, 0)),
            pl.BlockSpec((1, MLA_QPAD), lambda b, hh, i: (0, 0)),
            pl.BlockSpec((1, MLA_QPAD), lambda b, hh, i: (0, 0)),
        ],
        out_specs=pl.BlockSpec((tq, MLA_V), lambda b, hh, i: (b * nq + i, hh)),
        out_shape=jax.ShapeDtypeStruct((t, h * MLA_V), BF16),
        scratch_shapes=[
            pltpu.VMEM((s, MLA_QPAD), BF16),
            pltpu.VMEM((tq, 1), F32),
            pltpu.VMEM((tq, 1), F32),
            pltpu.VMEM((tq, MLA_V), F32),
        ],
        compiler_params=_params("parallel", "parallel", "arbitrary"),
        name="mla_attention",
    )(q, kv, kv, u, cos_t, sin_t, cos_t, sin_t, gq, gk)


def _rope_tables(s):
    half = MLA_ROPE // 2
    inv = ROPE_THETA ** (-jnp.arange(half, dtype=F32) / half)
    ang = jnp.arange(s).astype(F32)[:, None] * inv
    cos, sin = jnp.cos(ang), jnp.sin(ang)
    zeros = jnp.zeros((s, LANES - MLA_ROPE), F32)
    return (jnp.concatenate([cos, cos, zeros], axis=-1), jnp.concatenate([-sin, sin, zeros], axis=-1))


SWA_CHUNKS = SWA_GROUP * SWA_HEAD_DIM // LANES


def _swa_kernel(sinks_ref, q_ref, kp_ref, kc_ref, vp_ref, vc_ref, bias_ref, gq_ref, gk_ref, o_ref):
    kh = pl.program_id(1)
    n = pl.program_id(2)
    kv_half = kh % 2
    kv_is_low = kv_half == 0
    lane = lax.broadcasted_iota(jnp.int32, (BLOCK, LANES), 1)
    own = lane // HALF_LANES == kv_half
    scale = 1.0 / math.sqrt(SWA_HEAD_DIM)

    q = jnp.concatenate(
        [_rms_half_lanes(q_ref[:, c * LANES:(c + 1) * LANES].astype(F32), gq_ref[...]) * scale
         for c in range(SWA_CHUNKS)], axis=0).astype(BF16)

    def by_parity(x):
        mine = jnp.where(own, x, 0.0)
        other = pltpu.roll(mine, HALF_LANES, 1)
        return (jnp.where(kv_is_low, mine, other).astype(BF16), jnp.where(kv_is_low, other, mine).astype(BF16))

    kp = by_parity(_rms_half_lanes(kp_ref[...].astype(F32), gk_ref[...]))
    kc = by_parity(_rms_half_lanes(kc_ref[...].astype(F32), gk_ref[...]))
    vp = by_parity(vp_ref[...].astype(F32))
    vc = by_parity(vc_ref[...].astype(F32))

    row_chunk = lax.broadcasted_iota(jnp.int32, (SWA_CHUNKS * BLOCK, 1), 0) // BLOCK
    out = None
    for e in range(2):
        sink = jnp.zeros((SWA_CHUNKS * BLOCK, 1), F32)
        for c in range(SWA_CHUNKS):
            sink = jnp.where(row_chunk == c, sinks_ref[kh * SWA_GROUP + 2 * c + e], sink)
        s_prev = jnp.where(n > 0, _dot_nt(q, kp[e]) + bias_ref[0, e, 0], NEG_BIG)
        s_cur = _dot_nt(q, kc[e]) + bias_ref[0, e, 1]
        m = jnp.maximum(jnp.maximum(jnp.max(s_prev, axis=-1, keepdims=True),
                                    jnp.max(s_cur, axis=-1, keepdims=True)), sink)
        p_prev = jnp.exp(s_prev - m)
        p_cur = jnp.exp(s_cur - m)
        l = (jnp.sum(p_prev, axis=-1, keepdims=True) + jnp.sum(p_cur, axis=-1, keepdims=True)
             + jnp.exp(sink - m))
        o = (_dot(p_prev.astype(BF16), vp[e]) + _dot(p_cur.astype(BF16), vc[e])) / l
        out = o if out is None else out + o
    for c in range(SWA_CHUNKS):
        o_ref[:, c * LANES:(c + 1) * LANES] = out[c * BLOCK:(c + 1) * BLOCK].astype(o_ref.dtype)


def _t5_causal_bucket(dist):
    exact = REL_BUCKETS // 2
    d = np.maximum(dist, 0)
    log_b = exact + (np.log(np.maximum(d, 1) / exact) / np.log(REL_MAX_DIST / exact)
                     * (REL_BUCKETS - exact)).astype(np.int32)
    log_b = np.minimum(log_b, REL_BUCKETS - 1)
    return np.where(d < exact, d, log_b).astype(np.int32)


def _swa_bias_table(rel_bias):
    dist = np.arange(BLOCK)[:, None] + BLOCK - np.arange(2 * BLOCK)[None, :]
    in_window = (dist >= 0) & (dist < WINDOW)
    bias = rel_bias.astype(F32)[_t5_causal_bucket(dist)]
    bias = jnp.where(in_window[:, :, None], bias, NEG_BIG)
    bias = bias.reshape(BLOCK, 2, BLOCK, SWA_KV_HEADS, SWA_CHUNKS, 2)
    bias = bias.transpose(3, 5, 1, 4, 0, 2)
    return bias.reshape(SWA_KV_HEADS, 2, 2, SWA_CHUNKS * BLOCK, BLOCK)


def _swa_attention(u, q_norm, k_norm, sinks, rel_bias, batch):
    t = u.shape[0]
    s = t // batch
    nblk = s // BLOCK
    nq_cols = SWA_Q_HEADS * SWA_HEAD_DIM
    k_block0 = nq_cols // LANES
    v_block0 = (nq_cols + SWA_KV_HEADS * SWA_HEAD_DIM) // LANES
    group_cols = SWA_CHUNKS * LANES
    gq = jnp.tile(q_norm.reshape(1, SWA_HEAD_DIM), (1, 2))
    gk = jnp.tile(k_norm.reshape(1, SWA_HEAD_DIM), (1, 2))

    def prev_map(col0):
        return lambda b, kh, n: (b * nblk + jnp.maximum(n - 1, 0), col0 + kh // 2)

    def cur_map(col0):
        return lambda b, kh, n: (b * nblk + n, col0 + kh // 2)

    return pl.pallas_call(
        _swa_kernel,
        grid=(batch, SWA_KV_HEADS, nblk),
        in_specs=[
            pl.BlockSpec(memory_space=pltpu.SMEM),
            pl.BlockSpec((BLOCK, group_cols), lambda b, kh, n: (b * nblk + n, kh)),
            pl.BlockSpec((BLOCK, LANES), prev_map(k_block0)),
            pl.BlockSpec((BLOCK, LANES), cur_map(k_block0)),
            pl.BlockSpec((BLOCK, LANES), prev_map(v_block0)),
            pl.BlockSpec((BLOCK, LANES), cur_map(v_block0)),
            pl.BlockSpec((1, 2, 2, SWA_CHUNKS * BLOCK, BLOCK), lambda b, kh, n: (kh, 0, 0, 0, 0)),
            pl.BlockSpec((1, LANES), lambda b, kh, n: (0, 0)),
            pl.BlockSpec((1, LANES), lambda b, kh, n: (0, 0)),
        ],
        out_specs=pl.BlockSpec((BLOCK, group_cols), lambda b, kh, n: (b * nblk + n, kh)),
        out_shape=jax.ShapeDtypeStruct((t, nq_cols), BF16),
        compiler_params=_params("parallel", "parallel", "parallel"),
        name="swa_attention",
    )(sinks.astype(F32), u, u, u, u, u, _swa_bias_table(rel_bias), gq, gk)


def _fox_gate_kernel(u_ref, b_ref, o_ref):
    x = u_ref[...] + b_ref[...]
    log_f = -(jnp.maximum(-x, 0.0) + jnp.log1p(jnp.exp(-jnp.abs(x))))
    row = lax.broadcasted_iota(jnp.int32, log_f.shape, 0)
    c = log_f
    shift = 1
    while shift < c.shape[0]:
        c = c + jnp.where(row >= shift, pltpu.roll(c, shift, 0), 0.0)
        shift *= 2
    o_ref[...] = c


def _fox_gate_cumsum(u_f, b_f, batch):
    t = u_f.shape[0]
    s = t // batch
    return pl.pallas_call(
        _fox_gate_kernel,
        grid=(batch,),
        in_specs=[pl.BlockSpec((s, LANES), lambda b: (b, 0)), pl.BlockSpec((1, LANES), lambda b: (0, 0))],
        out_specs=pl.BlockSpec((s, LANES), lambda b: (b, 0)),
        out_shape=jax.ShapeDtypeStruct((t, LANES), F32),
        compiler_params=_params("parallel"),
        name="fox_gate_cumsum",
    )(u_f, b_f)


def _fox_attn_kernel(q_ref, k_ref, v_ref, cq_ref, ck_ref, gq_ref, gk_ref, o_ref,
                     k_scr, m_scr, l_scr, acc_scr, *, tq):
    hp = pl.program_id(1)
    i = pl.program_id(2)

    @pl.when(i == 0)
    def _():
        k_scr[...] = _rms_half_lanes(k_ref[...].astype(F32), gk_ref[...]).astype(BF16)

    scale = 1.0 / math.sqrt(FOX_HEAD_DIM)
    lane = lax.broadcasted_iota(jnp.int32, (tq, LANES), 1)
    low = lane < HALF_LANES
    qn = _rms_half_lanes(q_ref[...].astype(F32), gq_ref[...]) * scale
    q_heads = (jnp.where(low, qn, 0.0).astype(BF16), jnp.where(low, 0.0, qn).astype(BF16))
    cq = cq_ref[...]
    cq_heads = tuple(jnp.sum(jnp.where(lane == 2 * hp + e, cq, 0.0), axis=-1, keepdims=True) for e in range(2))

    m_scr[...] = jnp.full(m_scr.shape, NEG_BIG, F32)
    l_scr[...] = jnp.zeros(l_scr.shape, F32)
    acc_scr[...] = jnp.zeros(acc_scr.shape, F32)

    def chunk(j, masked):
        rows = pl.ds(pl.multiple_of(j * tq, tq), tq)
        k = k_scr[rows, :]
        v = v_ref[rows, :]
        for e in range(2):
            ck = ck_ref[0, 0, e, pl.ds(j, 1), :]
            s = _dot_nt(q_heads[e], k) + (cq_heads[e] - ck)
            if masked:
                s = _causal_mask(s)
            _online_softmax_step(s, v, m_scr.at[e], l_scr.at[e], acc_scr.at[e])

    def body(j, carry):
        chunk(j, False)
        return carry

    lax.fori_loop(0, i, body, 0)
    chunk(i, True)
    o_ref[...] = jnp.where(low, acc_scr[0] / l_scr[0], acc_scr[1] / l_scr[1]).astype(o_ref.dtype)


def _fox_attention(u, c, q_norm, k_norm, batch):
    t = u.shape[0]
    s = t // batch
    tq = 512 if s % 512 == 0 and s > 512 else BLOCK
    nq = s // tq
    pairs = FOX_HEADS // 2
    blocks = FOX_WIDTH // LANES
    c_keys = c.reshape(batch, s, LANES)[:, :, :FOX_HEADS].transpose(0, 2, 1).reshape(batch, pairs, 2, nq, tq)
    gq = jnp.tile(q_norm.reshape(1, FOX_HEAD_DIM), (1, 2))
    gk = jnp.tile(k_norm.reshape(1, FOX_HEAD_DIM), (1, 2))
    kernel = functools.partial(_fox_attn_kernel, tq=tq)
    return pl.pallas_call(
        kernel,
        grid=(batch, pairs, nq),
        in_specs=[
            pl.BlockSpec((tq, LANES), lambda b, hp, i: (b * nq + i, hp)),
            pl.BlockSpec((s, LANES), lambda b, hp, i: (b, blocks + hp)),
            pl.BlockSpec((s, LANES), lambda b, hp, i: (b, 2 * blocks + hp)),
            pl.BlockSpec((tq, LANES), lambda b, hp, i: (b * nq + i, 0)),
            pl.BlockSpec((1, 1, 2, nq, tq), lambda b, hp, i: (b, hp, 0, 0, 0)),
            pl.BlockSpec((1, LANES), lambda b, hp, i: (0, 0)),
            pl.BlockSpec((1, LANES), lambda b, hp, i: (0, 0)),
        ],
        out_specs=pl.BlockSpec((tq, LANES), lambda b, hp, i: (b * nq + i, hp)),
        out_shape=jax.ShapeDtypeStruct((t, FOX_WIDTH), BF16),
        scratch_shapes=[
            pltpu.VMEM((s, LANES), BF16),
            pltpu.VMEM((2, tq, 1), F32),
            pltpu.VMEM((2, tq, 1), F32),
            pltpu.VMEM((2, tq, LANES), F32),
        ],
        compiler_params=_params("parallel", "parallel", "arbitrary"),
        name="fox_attention",
    )(u, u, u, c, c_keys, gq, gk)


def kernel(x, mem, norm_ffn1, ffn1_w_gate, ffn1_w_up, ffn1_w_down, norm_mix, norm_ffn2, ffn2_w_gate, ffn2_w_up, ffn2_w_down, norm_mem, mem_w_kv, mem_q_norm, mem_k_norm, conv_w_in, conv_w, conv_w_out, mla_w_in, mla_q_a_norm, mla_w_q_b, mla_kv_a_norm, mla_w_kv_b, mla_q_norm, mla_k_norm, mla_w_out, swa_w_in, swa_q_norm, swa_k_norm, swa_sinks, swa_w_out, rel_bias, fox_w_in, fox_b_f, fox_q_norm, fox_k_norm, fox_w_out):
    b, s, d = x.shape
    m_len = mem.shape[1]
    depth = norm_ffn1.shape[0]
    xt = x.reshape(b * s, d)
    mem2 = mem.reshape(b * m_len, d)
    bf = lambda w: w.astype(BF16)

    for i in range(depth):
        kind, occ = i % 4, i // 4
        xt = _ffn(xt, norm_ffn1[i], bf(ffn1_w_gate[i]), bf(ffn1_w_up[i]), bf(ffn1_w_down[i]))

        if kind == 0:
            c = conv_w.shape[-1]
            u = _norm_matmul(xt, norm_mix[i], bf(conv_w_in[occ]), name="conv_in_proj")
            memq_block = 3 * c // MEM_WIDTH
            mix = _conv_mixer(u, conv_w[occ], b)
            w_out = conv_w_out[occ]
        elif kind == 1:
            w_in = mla_w_in[occ]
            lat = MLA_Q_RANK + MLA_KV_RANK
            w_in = jnp.concatenate([w_in[:, :lat], w_in[:, lat + MLA_ROPE:], w_in[:, lat:lat + MLA_ROPE],
                                    jnp.zeros((d, LANES - MLA_ROPE), F32)], axis=1)
            u = _norm_matmul(xt, norm_mix[i], bf(w_in), name="mla_in_proj")
            memq_block = lat // MEM_WIDTH
            krope_block = (lat + MEM_WIDTH) // LANES
            wq = mla_w_q_b[occ].reshape(MLA_Q_RANK, MLA_HEADS, MLA_QK)
            wq = jnp.concatenate([wq, jnp.zeros((MLA_Q_RANK, MLA_HEADS, MLA_QPAD - MLA_QK), F32)], axis=-1)
            wkv = mla_w_kv_b[occ].reshape(MLA_KV_RANK, MLA_HEADS, MLA_NOPE + MLA_V)
            wkv = jnp.concatenate([wkv[..., :MLA_NOPE].reshape(MLA_KV_RANK, -1),
                                   wkv[..., MLA_NOPE:].reshape(MLA_KV_RANK, -1)], axis=1)
            q = _norm_matmul(u, mla_q_a_norm[occ], bf(wq.reshape(MLA_Q_RANK, -1)), col_block=0, name="mla_q_proj")
            kv = _norm_matmul(u, mla_kv_a_norm[occ], bf(wkv), col_block=1, name="mla_kv_proj")
            pad = jnp.zeros((MLA_QPAD - MLA_QK,), F32)
            gq = jnp.concatenate([mla_q_norm[occ], pad]).reshape(1, MLA_QPAD)
            gk = jnp.concatenate([mla_k_norm[occ], pad]).reshape(1, MLA_QPAD)
            cos_t, sin_t = _rope_tables(s)
            mix = _mla_attention(q, kv, u, krope_block, cos_t, sin_t, gq, gk, b)
            w_out = mla_w_out[occ]
        elif kind == 2:
            u = _norm_matmul(xt, norm_mix[i], bf(swa_w_in[occ]), name="swa_in_proj")
            memq_block = (SWA_Q_HEADS + 2 * SWA_KV_HEADS) * SWA_HEAD_DIM // MEM_WIDTH
            mix = _swa_attention(u, swa_q_norm[occ], swa_k_norm[occ], swa_sinks[occ], rel_bias, b)
            w_out = swa_w_out[occ]
        else:
            w_in = fox_w_in[occ]
            qkv = 3 * FOX_WIDTH
            w_main = jnp.concatenate([w_in[:, :qkv], w_in[:, qkv + FOX_HEADS:]], axis=1)
            w_gate = jnp.concatenate([w_in[:, qkv:qkv + FOX_HEADS], jnp.zeros((d, LANES - FOX_HEADS), F32)], axis=1)
            u = _norm_matmul(xt, norm_mix[i], bf(w_main), name="fox_in_proj")
            u_f = _norm_matmul(xt, norm_mix[i], bf(w_gate), out_dtype=F32, name="fox_gate_proj")
            b_f = jnp.concatenate([fox_b_f[occ], jnp.zeros((LANES - FOX_HEADS,), F32)]).reshape(1, LANES)
            c_gate = _fox_gate_cumsum(u_f, b_f, b)
            memq_block = qkv // MEM_WIDTH
            mix = _fox_attention(u, c_gate, fox_q_norm[occ], fox_k_norm[occ], b)
            w_out = fox_w_out[occ]

        mkv = _norm_matmul(mem2, norm_mem[i], bf(mem_w_kv[i]), name="mem_kv_proj")
        mem_out = _mem_attn(u, memq_block, mkv, mem_q_norm[i], mem_k_norm[i], b)
        k_mix = mix.shape[1]
        xt = _out_proj(mix, mem_out, bf(w_out[:k_mix]), bf(w_out[k_mix:]), xt)

        xt = _ffn(xt, norm_ffn2[i], bf(ffn2_w_gate[i]), bf(ffn2_w_up[i]), bf(ffn2_w_down[i]))
    return xt.reshape(b, s, d)
```

```python
import functools
import math

import numpy as np
import jax
import jax.numpy as jnp
from jax import lax
from jax.experimental import pallas as pl
from jax.experimental.pallas import tpu as pltpu

F32 = jnp.float32
BF16 = jnp.bfloat16

RMS_EPS = 1e-6
NEG_BIG = -1e30
LOG2_E = math.log2(math.e)
LANES = 128
HALF_LANES = LANES // 2
VMEM_LIMIT_BYTES = 56 * 1024 * 1024

BLOCK = 128
MEM_HEADS = 4
MEM_HEAD_DIM = 128
MEM_WIDTH = MEM_HEADS * MEM_HEAD_DIM
CONV_K = 3
MLA_HEADS = 16
MLA_Q_RANK = 512
MLA_KV_RANK = 512
MLA_NOPE = 128
MLA_ROPE = 64
MLA_V = 128
MLA_QK = MLA_NOPE + MLA_ROPE
MLA_QPAD = 2 * LANES
ROPE_THETA = 10000.0
SWA_Q_HEADS = 32
SWA_KV_HEADS = 4
SWA_GROUP = SWA_Q_HEADS // SWA_KV_HEADS
SWA_HEAD_DIM = 64
WINDOW = 128
REL_BUCKETS = 32
REL_MAX_DIST = 128
FOX_HEADS = 32
FOX_HEAD_DIM = 64
FOX_WIDTH = FOX_HEADS * FOX_HEAD_DIM


def _params(*semantics):
    return pltpu.CompilerParams(dimension_semantics=semantics, vmem_limit_bytes=VMEM_LIMIT_BYTES)


def _pick_tile(n, target):
    best = None
    for t in range(LANES, min(n, target) + 1, LANES):
        if n % t == 0:
            best = t
    return n if best is None else best


def _rms_rows(x, gain):
    r = lax.rsqrt(jnp.mean(x * x, axis=-1, keepdims=True) + RMS_EPS)
    return x * r * gain


def _rms_half_lanes(x, gain):
    lane = lax.broadcasted_iota(jnp.int32, x.shape, 1)
    low = lane < HALF_LANES
    sq = x * x
    s_low = jnp.sum(jnp.where(low, sq, 0.0), axis=-1, keepdims=True)
    s_high = jnp.sum(jnp.where(low, 0.0, sq), axis=-1, keepdims=True)
    r = lax.rsqrt(jnp.where(low, s_low, s_high) * (1.0 / HALF_LANES) + RMS_EPS)
    return x * r * gain


def _dot(a, b):
    return jnp.dot(a, b, preferred_element_type=F32)


def _dot_nt(a, b):
    return lax.dot_general(a, b, (((1,), (1,)), ((), ())), preferred_element_type=F32)


def _ffn_kernel(x_ref, g_ref, wg_ref, wu_ref, wd_ref, o_ref, h_scr, acc_scr):
    f = pl.program_id(1)

    @pl.when(f == 0)
    def _():
        h_scr[...] = _rms_rows(x_ref[...], g_ref[...]).astype(BF16)

    h = h_scr[...]
    gate = _dot(h, wg_ref[...])
    up = _dot(h, wu_ref[...])
    act = (gate / (1.0 + jnp.exp(-gate)) * up).astype(BF16)
    part = _dot(act, wd_ref[...])

    @pl.when(f == 0)
    def _():
        acc_scr[...] = part

    @pl.when(f > 0)
    def _():
        acc_scr[...] += part

    @pl.when(f == pl.num_programs(1) - 1)
    def _():
        o_ref[...] = x_ref[...] + 0.5 * acc_scr[...]


def _ffn(x, gain, w_gate, w_up, w_down):
    t, d = x.shape
    f = w_gate.shape[1]
    tm = _pick_tile(t, 512)
    tf = _pick_tile(f, 512)
    return pl.pallas_call(
        _ffn_kernel,
        grid=(t // tm, f // tf),
        in_specs=[
            pl.BlockSpec((tm, d), lambda i, j: (i, 0)),
            pl.BlockSpec((1, d), lambda i, j: (0, 0)),
            pl.BlockSpec((d, tf), lambda i, j: (0, j)),
            pl.BlockSpec((d, tf), lambda i, j: (0, j)),
            pl.BlockSpec((tf, d), lambda i, j: (j, 0)),
        ],
        out_specs=pl.BlockSpec((tm, d), lambda i, j: (i, 0)),
        out_shape=jax.ShapeDtypeStruct((t, d), F32),
        scratch_shapes=[pltpu.VMEM((tm, d), BF16), pltpu.VMEM((tm, d), F32)],
        compiler_params=_params("parallel", "arbitrary"),
        name="ffn",
    )(x, gain.reshape(1, d), w_gate, w_up, w_down)


def _norm_matmul_kernel(x_ref, g_ref, w_ref, o_ref, h_scr):
    @pl.when(pl.program_id(1) == 0)
    def _():
        h_scr[...] = _rms_rows(x_ref[...].astype(F32), g_ref[...]).astype(BF16)

    o_ref[...] = _dot(h_scr[...], w_ref[...]).astype(o_ref.dtype)


def _norm_matmul(x, gain, w, *, col_block=0, out_dtype=BF16, name="norm_matmul"):
    m = x.shape[0]
    k, n = w.shape
    tm = _pick_tile(m, 512)
    tn = n if n <= 2048 else _pick_tile(n, 1024)
    return pl.pallas_call(
        _norm_matmul_kernel,
        grid=(m // tm, n // tn),
        in_specs=[
            pl.BlockSpec((tm, k), lambda i, j: (i, col_block)),
            pl.BlockSpec((1, k), lambda i, j: (0, 0)),
            pl.BlockSpec((k, tn), lambda i, j: (0, j)),
        ],
        out_specs=pl.BlockSpec((tm, tn), lambda i, j: (i, j)),
        out_shape=jax.ShapeDtypeStruct((m, n), out_dtype),
        scratch_shapes=[pltpu.VMEM((tm, k), BF16)],
        compiler_params=_params("parallel", "arbitrary"),
        name=name,
    )(x, gain.reshape(1, k), w)


def _out_proj_kernel(a_ref, b_ref, wa_ref, wb_ref, x_ref, o_ref):
    acc = _dot(a_ref[...], wa_ref[...])
    acc = acc + _dot(b_ref[...], wb_ref[...])
    o_ref[...] = x_ref[...] + acc


def _out_proj(mix, mem_out, w_mix, w_mem, x):
    t, d = x.shape
    ka = mix.shape[1]
    kb = mem_out.shape[1]
    tm = _pick_tile(t, 512)
    tn = _pick_tile(d, 1024)
    return pl.pallas_call(
        _out_proj_kernel,
        grid=(t // tm, d // tn),
        in_specs=[
            pl.BlockSpec((tm, ka), lambda i, j: (i, 0)),
            pl.BlockSpec((tm, kb), lambda i, j: (i, 0)),
            pl.BlockSpec((ka, tn), lambda i, j: (0, j)),
            pl.BlockSpec((kb, tn), lambda i, j: (0, j)),
            pl.BlockSpec((tm, tn), lambda i, j: (i, j)),
        ],
        out_specs=pl.BlockSpec((tm, tn), lambda i, j: (i, j)),
        out_shape=jax.ShapeDtypeStruct((t, d), F32),
        compiler_params=_params("parallel", "parallel"),
        name="out_proj",
    )(mix, mem_out, w_mix, w_mem, x)


def _mem_attn_kernel(q_ref, kv_ref, gq_ref, gk_ref, o_ref):
    scale = 1.0 / math.sqrt(MEM_HEAD_DIM)
    for h in range(MEM_HEADS):
        cols = slice(h * MEM_HEAD_DIM, (h + 1) * MEM_HEAD_DIM)
        vcols = slice(MEM_WIDTH + h * MEM_HEAD_DIM, MEM_WIDTH + (h + 1) * MEM_HEAD_DIM)
        q = (_rms_rows(q_ref[:, cols].astype(F32), gq_ref[...]) * scale).astype(BF16)
        k = _rms_rows(kv_ref[:, cols].astype(F32), gk_ref[...]).astype(BF16)
        s = _dot_nt(q, k)
        m = jnp.max(s, axis=-1, keepdims=True)
        p = jnp.exp(s - m)
        l = jnp.sum(p, axis=-1, keepdims=True)
        o = _dot(p.astype(BF16), kv_ref[:, vcols]) / l
        o_ref[:, cols] = o.astype(o_ref.dtype)


def _mem_attn(u, q_col_block, mkv, gq, gk, batch):
    t = u.shape[0]
    s = t // batch
    m_len = mkv.shape[0] // batch
    tq = _pick_tile(s, 512)
    nq = s // tq
    return pl.pallas_call(
        _mem_attn_kernel,
        grid=(batch, nq),
        in_specs=[
            pl.BlockSpec((tq, MEM_WIDTH), lambda b, i: (b * nq + i, q_col_block)),
            pl.BlockSpec((m_len, 2 * MEM_WIDTH), lambda b, i: (b, 0)),
            pl.BlockSpec((1, MEM_HEAD_DIM), lambda b, i: (0, 0)),
            pl.BlockSpec((1, MEM_HEAD_DIM), lambda b, i: (0, 0)),
        ],
        out_specs=pl.BlockSpec((tq, MEM_WIDTH), lambda b, i: (b * nq + i, 0)),
        out_shape=jax.ShapeDtypeStruct((t, MEM_WIDTH), BF16),
        compiler_params=_params("parallel", "parallel"),
        name="mem_attn",
    )(u, mkv, gq.reshape(1, MEM_HEAD_DIM), gk.reshape(1, MEM_HEAD_DIM))


CONV_HALO = 16


def _conv_kernel(gb_ref, gc_ref, xt_ref, gch_ref, xth_ref, w_ref, o_ref, *, tiles_per_seq):
    i = pl.program_id(0)
    z = gc_ref[...].astype(F32) * xt_ref[...].astype(F32)
    zh = gch_ref[...].astype(F32) * xth_ref[...].astype(F32)
    zh = jnp.where(i % tiles_per_seq == 0, 0.0, zh)
    zc = jnp.concatenate([zh[CONV_HALO - 8:], z], axis=0)
    z1 = pltpu.roll(zc, 1, 0)[8:]
    z2 = pltpu.roll(zc, 2, 0)[8:]
    w = w_ref[...]
    conv = z2 * w[0:1] + z1 * w[1:2] + z * w[2:3]
    o_ref[...] = (gb_ref[...].astype(F32) * conv).astype(o_ref.dtype)


def _conv_mixer(u, conv_w, batch):
    t = u.shape[0]
    s = t // batch
    c = conv_w.shape[1]
    ts = _pick_tile(s, 512)
    tc = _pick_tile(c, 512)
    nc = c // tc
    halo_per_tile = ts // CONV_HALO

    def halo_map(col0):
        return lambda i, j: (jnp.maximum(i * halo_per_tile - 1, 0), col0 + j)

    return pl.pallas_call(
        functools.partial(_conv_kernel, tiles_per_seq=s // ts),
        grid=(t // ts, nc),
        in_specs=[
            pl.BlockSpec((ts, tc), lambda i, j: (i, j)),
            pl.BlockSpec((ts, tc), lambda i, j: (i, nc + j)),
            pl.BlockSpec((ts, tc), lambda i, j: (i, 2 * nc + j)),
            pl.BlockSpec((CONV_HALO, tc), halo_map(nc)),
            pl.BlockSpec((CONV_HALO, tc), halo_map(2 * nc)),
            pl.BlockSpec((CONV_K, tc), lambda i, j: (0, j)),
        ],
        out_specs=pl.BlockSpec((ts, tc), lambda i, j: (i, j)),
        out_shape=jax.ShapeDtypeStruct((t, c), BF16),
        compiler_params=_params("parallel", "parallel"),
        name="conv_mixer",
    )(u, u, u, u, u, conv_w)


def _online_softmax_step(s, v, m_ref, l_ref, acc_ref):
    m_prev = m_ref[...]
    m_new = jnp.maximum(m_prev, jnp.max(s, axis=-1, keepdims=True))
    alpha = jnp.exp2(m_prev - m_new)
    p = [jnp.exp2(s[:, t * LANES:(t + 1) * LANES] - m_new) for t in range(s.shape[1] // LANES)]
    l_ref[...] = alpha * l_ref[...] + functools.reduce(lambda a, b: a + b, p)
    acc_ref[...] = alpha * acc_ref[...] + _dot(jnp.concatenate(p, axis=-1).astype(BF16), v)
    m_ref[...] = m_new


def _softmax_finish(l_ref, acc_ref):
    return acc_ref[...] / jnp.sum(l_ref[...], axis=-1, keepdims=True)


def _causal_mask(s):
    row = lax.broadcasted_iota(jnp.int32, s.shape, 0)
    col = lax.broadcasted_iota(jnp.int32, s.shape, 1)
    return jnp.where(col <= row, s, NEG_BIG)


def _swap_rope_halves(t):
    lane = lax.broadcasted_iota(jnp.int32, t.shape, 1)
    up = pltpu.roll(t, MLA_ROPE // 2, 1)
    down = pltpu.roll(t, LANES - MLA_ROPE // 2, 1)
    return jnp.where(lane < MLA_ROPE // 2, down, up)


def _rope_lanes(t, cos_t, sin_t):
    return t * cos_t + _swap_rope_halves(t) * sin_t


def _rms_rope_part(x, gain):
    r = lax.rsqrt(jnp.sum(x * x, axis=-1, keepdims=True) * (1.0 / MLA_ROPE) + RMS_EPS)
    return x * r * gain


def _mla_attn_kernel(q_ref, kn_ref, v_ref, kr_ref, cosq_ref, sinq_ref, cosk_ref, sink_ref,
                     gq_ref, gk_ref, o_ref, k_scr, m_scr, l_scr, acc_scr, *, tq):
    i = pl.program_id(2)

    @pl.when(i == 0)
    def _():
        k_scr[:, :LANES] = _rms_rows(kn_ref[...].astype(F32), gk_ref[:, :LANES]).astype(BF16)
        kr = _rms_rope_part(kr_ref[...].astype(F32), gk_ref[:, LANES:])
        k_scr[:, LANES:] = _rope_lanes(kr, cosk_ref[...], sink_ref[...]).astype(BF16)

    scale = LOG2_E / math.sqrt(MLA_QK)
    q_nope = _rms_rows(q_ref[:, :LANES].astype(F32), gq_ref[:, :LANES])
    q_rot = _rms_rope_part(q_ref[:, LANES:].astype(F32), gq_ref[:, LANES:])
    q_rot = _rope_lanes(q_rot, cosq_ref[...], sinq_ref[...])
    q = (jnp.concatenate([q_nope, q_rot], axis=-1) * scale).astype(BF16)

    m_scr[...] = jnp.full(m_scr.shape, NEG_BIG, F32)
    l_scr[...] = jnp.zeros(l_scr.shape, F32)
    acc_scr[...] = jnp.zeros(acc_scr.shape, F32)

    def chunk(j, masked):
        rows = pl.ds(pl.multiple_of(j * tq, tq), tq)
        s = _dot_nt(q, k_scr[rows, :])
        if masked:
            s = _causal_mask(s)
        _online_softmax_step(s, v_ref[rows, :], m_scr, l_scr, acc_scr)

    def body(j, carry):
        chunk(j, False)
        return carry

    lax.fori_loop(0, i, body, 0)
    chunk(i, True)
    o_ref[...] = _softmax_finish(l_scr, acc_scr).astype(o_ref.dtype)


def _mla_attention(q, kv, u, krope_block, cos_t, sin_t, gq, gk, batch):
    t = q.shape[0]
    s = t // batch
    tq = 512 if s % 512 == 0 and s > 512 else BLOCK
    nq = s // tq
    h = MLA_HEADS
    kernel = functools.partial(_mla_attn_kernel, tq=tq)
    return pl.pallas_call(
        kernel,
        grid=(batch, h, nq),
        in_specs=[
            pl.BlockSpec((tq, MLA_QPAD), lambda b, hh, i: (b * nq + i, hh)),
            pl.BlockSpec((s, LANES), lambda b, hh, i: (b, hh)),
            pl.BlockSpec((s, LANES), lambda b, hh, i: (b, h + hh)),
            pl.BlockSpec((s, LANES), lambda b, hh, i: (b, krope_block)),
            pl.BlockSpec((tq, LANES), lambda b, hh, i: (i, 0)),
            pl.BlockSpec((tq, LANES), lambda b, hh, i: (i, 0)),
            pl.BlockSpec((s, LANES), lambda b, hh, i: (0, 0)),
            pl.BlockSpec((s, LANES), lambda b, hh, i: (0, 0)),
            pl.BlockSpec((1, MLA_QPAD), lambda b, hh, i: (0, 0)),
            pl.BlockSpec((1, MLA_QPAD), lambda b, hh, i: (0, 0)),
        ],
        out_specs=pl.BlockSpec((tq, MLA_V), lambda b, hh, i: (b * nq + i, hh)),
        out_shape=jax.ShapeDtypeStruct((t, h * MLA_V), BF16),
        scratch_shapes=[
            pltpu.VMEM((s, MLA_QPAD), BF16),
            pltpu.VMEM((tq, LANES), F32),
            pltpu.VMEM((tq, LANES), F32),
            pltpu.VMEM((tq, MLA_V), F32),
        ],
        compiler_params=_params("parallel", "parallel", "arbitrary"),
        name="mla_attention",
    )(q, kv, kv, u, cos_t, sin_t, cos_t, sin_t, gq, gk)


def _rope_tables(s):
    half = MLA_ROPE // 2
    inv = ROPE_THETA ** (-jnp.arange(half, dtype=F32) / half)
    ang = jnp.arange(s).astype(F32)[:, None] * inv
    cos, sin = jnp.cos(ang), jnp.sin(ang)
    zeros = jnp.zeros((s, LANES - MLA_ROPE), F32)
    return (jnp.concatenate([cos, cos, zeros], axis=-1), jnp.concatenate([-sin, sin, zeros], axis=-1))


SWA_CHUNKS = SWA_GROUP * SWA_HEAD_DIM // LANES


def _swa_kernel(sinks_ref, q_ref, kp_ref, kc_ref, vp_ref, vc_ref, bias_ref, gq_ref, gk_ref, o_ref):
    kh = pl.program_id(1)
    n = pl.program_id(2)
    kv_half = kh % 2
    kv_is_low = kv_half == 0
    lane = lax.broadcasted_iota(jnp.int32, (BLOCK, LANES), 1)
    own = lane // HALF_LANES == kv_half
    scale = 1.0 / math.sqrt(SWA_HEAD_DIM)

    q = jnp.concatenate(
        [_rms_half_lanes(q_ref[:, c * LANES:(c + 1) * LANES].astype(F32), gq_ref[...]) * scale
         for c in range(SWA_CHUNKS)], axis=0).astype(BF16)

    def by_parity(x):
        mine = jnp.where(own, x, 0.0)
        other = pltpu.roll(mine, HALF_LANES, 1)
        return (jnp.where(kv_is_low, mine, other).astype(BF16), jnp.where(kv_is_low, other, mine).astype(BF16))

    kp = by_parity(_rms_half_lanes(kp_ref[...].astype(F32), gk_ref[...]))
    kc = by_parity(_rms_half_lanes(kc_ref[...].astype(F32), gk_ref[...]))
    vp = by_parity(vp_ref[...].astype(F32))
    vc = by_parity(vc_ref[...].astype(F32))

    row_chunk = lax.broadcasted_iota(jnp.int32, (SWA_CHUNKS * BLOCK, 1), 0) // BLOCK
    out = None
    for e in range(2):
        sink = jnp.zeros((SWA_CHUNKS * BLOCK, 1), F32)
        for c in range(SWA_CHUNKS):
            sink = jnp.where(row_chunk == c, sinks_ref[kh * SWA_GROUP + 2 * c + e], sink)
        s_prev = jnp.where(n > 0, _dot_nt(q, kp[e]) + bias_ref[0, e, 0], NEG_BIG)
        s_cur = _dot_nt(q, kc[e]) + bias_ref[0, e, 1]
        m = jnp.maximum(jnp.maximum(jnp.max(s_prev, axis=-1, keepdims=True),
                                    jnp.max(s_cur, axis=-1, keepdims=True)), sink)
        p_prev = jnp.exp(s_prev - m)
        p_cur = jnp.exp(s_cur - m)
        l = (jnp.sum(p_prev, axis=-1, keepdims=True) + jnp.sum(p_cur, axis=-1, keepdims=True)
             + jnp.exp(sink - m))
        o = (_dot(p_prev.astype(BF16), vp[e]) + _dot(p_cur.astype(BF16), vc[e])) / l
        out = o if out is None else out + o
    for c in range(SWA_CHUNKS):
        o_ref[:, c * LANES:(c + 1) * LANES] = out[c * BLOCK:(c + 1) * BLOCK].astype(o_ref.dtype)


def _t5_causal_bucket(dist):
    exact = REL_BUCKETS // 2
    d = np.maximum(dist, 0)
    log_b = exact + (np.log(np.maximum(d, 1) / exact) / np.log(REL_MAX_DIST / exact)
                     * (REL_BUCKETS - exact)).astype(np.int32)
    log_b = np.minimum(log_b, REL_BUCKETS - 1)
    return np.where(d < exact, d, log_b).astype(np.int32)


def _swa_bias_table(rel_bias):
    dist = np.arange(BLOCK)[:, None] + BLOCK - np.arange(2 * BLOCK)[None, :]
    in_window = (dist >= 0) & (dist < WINDOW)
    bias = rel_bias.astype(F32)[_t5_causal_bucket(dist)]
    bias = jnp.where(in_window[:, :, None], bias, NEG_BIG)
    bias = bias.reshape(BLOCK, 2, BLOCK, SWA_KV_HEADS, SWA_CHUNKS, 2)
    bias = bias.transpose(3, 5, 1, 4, 0, 2)
    return bias.reshape(SWA_KV_HEADS, 2, 2, SWA_CHUNKS * BLOCK, BLOCK)


def _swa_attention(u, q_norm, k_norm, sinks, rel_bias, batch):
    t = u.shape[0]
    s = t // batch
    nblk = s // BLOCK
    nq_cols = SWA_Q_HEADS * SWA_HEAD_DIM
    k_block0 = nq_cols // LANES
    v_block0 = (nq_cols + SWA_KV_HEADS * SWA_HEAD_DIM) // LANES
    group_cols = SWA_CHUNKS * LANES
    gq = jnp.tile(q_norm.reshape(1, SWA_HEAD_DIM), (1, 2))
    gk = jnp.tile(k_norm.reshape(1, SWA_HEAD_DIM), (1, 2))

    def prev_map(col0):
        return lambda b, kh, n: (b * nblk + jnp.maximum(n - 1, 0), col0 + kh // 2)

    def cur_map(col0):
        return lambda b, kh, n: (b * nblk + n, col0 + kh // 2)

    return pl.pallas_call(
        _swa_kernel,
        grid=(batch, SWA_KV_HEADS, nblk),
        in_specs=[
            pl.BlockSpec(memory_space=pltpu.SMEM),
            pl.BlockSpec((BLOCK, group_cols), lambda b, kh, n: (b * nblk + n, kh)),
            pl.BlockSpec((BLOCK, LANES), prev_map(k_block0)),
            pl.BlockSpec((BLOCK, LANES), cur_map(k_block0)),
            pl.BlockSpec((BLOCK, LANES), prev_map(v_block0)),
            pl.BlockSpec((BLOCK, LANES), cur_map(v_block0)),
            pl.BlockSpec((1, 2, 2, SWA_CHUNKS * BLOCK, BLOCK), lambda b, kh, n: (kh, 0, 0, 0, 0)),
            pl.BlockSpec((1, LANES), lambda b, kh, n: (0, 0)),
            pl.BlockSpec((1, LANES), lambda b, kh, n: (0, 0)),
        ],
        out_specs=pl.BlockSpec((BLOCK, group_cols), lambda b, kh, n: (b * nblk + n, kh)),
        out_shape=jax.ShapeDtypeStruct((t, nq_cols), BF16),
        compiler_params=_params("parallel", "parallel", "parallel"),
        name="swa_attention",
    )(sinks.astype(F32), u, u, u, u, u, _swa_bias_table(rel_bias), gq, gk)


def _fox_gate_kernel(u_ref, b_ref, o_ref):
    x = u_ref[...] + b_ref[...]
    log_f = -(jnp.maximum(-x, 0.0) + jnp.log1p(jnp.exp(-jnp.abs(x))))
    row = lax.broadcasted_iota(jnp.int32, log_f.shape, 0)
    c = log_f
    shift = 1
    while shift < c.shape[0]:
        c = c + jnp.where(row >= shift, pltpu.roll(c, shift, 0), 0.0)
        shift *= 2
    o_ref[...] = c


def _fox_gate_cumsum(u_f, b_f, batch):
    t = u_f.shape[0]
    s = t // batch
    return pl.pallas_call(
        _fox_gate_kernel,
        grid=(batch,),
        in_specs=[pl.BlockSpec((s, LANES), lambda b: (b, 0)), pl.BlockSpec((1, LANES), lambda b: (0, 0))],
        out_specs=pl.BlockSpec((s, LANES), lambda b: (b, 0)),
        out_shape=jax.ShapeDtypeStruct((t, LANES), F32),
        compiler_params=_params("parallel"),
        name="fox_gate_cumsum",
    )(u_f, b_f)


FOX_GATE_PIECES = 3


def _split_bf16_pieces(x):
    pieces = []
    for _ in range(FOX_GATE_PIECES - 1):
        piece = x.astype(BF16).astype(F32)
        pieces.append(piece)
        x = x - piece
    return pieces + [x]


def _fox_operand(x, gate_col, lane, head, is_query):
    base = (1 - head) * HALF_LANES
    out = jnp.where(lane // HALF_LANES == head, x, 0.0)
    for t, piece in enumerate(_split_bf16_pieces(gate_col)):
        gate_lane, one_lane = base + t, base + FOX_GATE_PIECES + t
        if not is_query:
            gate_lane, one_lane, piece = one_lane, gate_lane, -piece
        out = jnp.where(lane == gate_lane, piece, out)
        out = jnp.where(lane == one_lane, 1.0, out)
    return out.astype(BF16)


def _fox_attn_kernel(q_ref, k_ref, v_ref, cq_ref, ck_ref, gq_ref, gk_ref, o_ref,
                     k_scr, m0_scr, l0_scr, acc0_scr, m1_scr, l1_scr, acc1_scr, *, tq):
    hp = pl.program_id(1)
    i = pl.program_id(2)

    def gate_col(c, e):
        lane = lax.broadcasted_iota(jnp.int32, c.shape, 1)
        return jnp.sum(jnp.where(lane == 2 * hp + e, c, 0.0), axis=-1, keepdims=True) * LOG2_E

    @pl.when(i == 0)
    def _():
        kn = _rms_half_lanes(k_ref[...].astype(F32), gk_ref[...])
        lane = lax.broadcasted_iota(jnp.int32, kn.shape, 1)
        for e in range(2):
            k_scr[e] = _fox_operand(kn, gate_col(ck_ref[...], e), lane, e, False)

    scale = LOG2_E / math.sqrt(FOX_HEAD_DIM)
    lane = lax.broadcasted_iota(jnp.int32, (tq, LANES), 1)
    qn = _rms_half_lanes(q_ref[...].astype(F32), gq_ref[...]) * scale
    q_heads = [_fox_operand(qn, gate_col(cq_ref[...], e), lane, e, True) for e in range(2)]

    state = ((m0_scr, l0_scr, acc0_scr), (m1_scr, l1_scr, acc1_scr))
    for m_scr, l_scr, acc_scr in state:
        m_scr[...] = jnp.full(m_scr.shape, NEG_BIG, F32)
        l_scr[...] = jnp.zeros(l_scr.shape, F32)
        acc_scr[...] = jnp.zeros(acc_scr.shape, F32)

    def chunk(j, masked):
        rows = pl.ds(pl.multiple_of(j * tq, tq), tq)
        v = v_ref[rows, :]
        for e in range(2):
            s = _dot_nt(q_heads[e], k_scr[e, rows, :])
            if masked:
                s = _causal_mask(s)
            _online_softmax_step(s, v, *state[e])

    def body(j, carry):
        chunk(j, False)
        return carry

    lax.fori_loop(0, i, body, 0)
    chunk(i, True)
    o_ref[...] = jnp.where(lane < HALF_LANES, _softmax_finish(l0_scr, acc0_scr),
                           _softmax_finish(l1_scr, acc1_scr)).astype(o_ref.dtype)


def _fox_attention(u, c, q_norm, k_norm, batch):
    t = u.shape[0]
    s = t // batch
    tq = 512 if s % 512 == 0 and s > 512 else BLOCK
    nq = s // tq
    pairs = FOX_HEADS // 2
    blocks = FOX_WIDTH // LANES
    gq = jnp.tile(q_norm.reshape(1, FOX_HEAD_DIM), (1, 2))
    gk = jnp.tile(k_norm.reshape(1, FOX_HEAD_DIM), (1, 2))
    kernel = functools.partial(_fox_attn_kernel, tq=tq)
    return pl.pallas_call(
        kernel,
        grid=(batch, pairs, nq),
        in_specs=[
            pl.BlockSpec((tq, LANES), lambda b, hp, i: (b * nq + i, hp)),
            pl.BlockSpec((s, LANES), lambda b, hp, i: (b, blocks + hp)),
            pl.BlockSpec((s, LANES), lambda b, hp, i: (b, 2 * blocks + hp)),
            pl.BlockSpec((tq, LANES), lambda b, hp, i: (b * nq + i, 0)),
            pl.BlockSpec((s, LANES), lambda b, hp, i: (b, 0)),
            pl.BlockSpec((1, LANES), lambda b, hp, i: (0, 0)),
            pl.BlockSpec((1, LANES), lambda b, hp, i: (0, 0)),
        ],
        out_specs=pl.BlockSpec((tq, LANES), lambda b, hp, i: (b * nq + i, hp)),
        out_shape=jax.ShapeDtypeStruct((t, FOX_WIDTH), BF16),
        scratch_shapes=[pltpu.VMEM((2, s, LANES), BF16)] + 2 * [
            pltpu.VMEM((tq, LANES), F32),
            pltpu.VMEM((tq, LANES), F32),
            pltpu.VMEM((tq, LANES), F32),
        ],
        compiler_params=_params("parallel", "parallel", "arbitrary"),
        name="fox_attention",
    )(u, u, u, c, c, gq, gk)


def kernel(x, mem, norm_ffn1, ffn1_w_gate, ffn1_w_up, ffn1_w_down, norm_mix, norm_ffn2, ffn2_w_gate, ffn2_w_up, ffn2_w_down, norm_mem, mem_w_kv, mem_q_norm, mem_k_norm, conv_w_in, conv_w, conv_w_out, mla_w_in, mla_q_a_norm, mla_w_q_b, mla_kv_a_norm, mla_w_kv_b, mla_q_norm, mla_k_norm, mla_w_out, swa_w_in, swa_q_norm, swa_k_norm, swa_sinks, swa_w_out, rel_bias, fox_w_in, fox_b_f, fox_q_norm, fox_k_norm, fox_w_out):
    b, s, d = x.shape
    m_len = mem.shape[1]
    depth = norm_ffn1.shape[0]
    xt = x.reshape(b * s, d)
    mem2 = mem.reshape(b * m_len, d)
    bf = lambda w: w.astype(BF16)

    for i in range(depth):
        kind, occ = i % 4, i // 4
        xt = _ffn(xt, norm_ffn1[i], bf(ffn1_w_gate[i]), bf(ffn1_w_up[i]), bf(ffn1_w_down[i]))

        if kind == 0:
            c = conv_w.shape[-1]
            u = _norm_matmul(xt, norm_mix[i], bf(conv_w_in[occ]), name="conv_in_proj")
            memq_block = 3 * c // MEM_WIDTH
            mix = _conv_mixer(u, conv_w[occ], b)
            w_out = conv_w_out[occ]
        elif kind == 1:
            w_in = mla_w_in[occ]
            lat = MLA_Q_RANK + MLA_KV_RANK
            w_in = jnp.concatenate([w_in[:, :lat], w_in[:, lat + MLA_ROPE:], w_in[:, lat:lat + MLA_ROPE],
                                    jnp.zeros((d, LANES - MLA_ROPE), F32)], axis=1)
            u = _norm_matmul(xt, norm_mix[i], bf(w_in), name="mla_in_proj")
            memq_block = lat // MEM_WIDTH
            krope_block = (lat + MEM_WIDTH) // LANES
            wq = mla_w_q_b[occ].reshape(MLA_Q_RANK, MLA_HEADS, MLA_QK)
            wq = jnp.concatenate([wq, jnp.zeros((MLA_Q_RANK, MLA_HEADS, MLA_QPAD - MLA_QK), F32)], axis=-1)
            wkv = mla_w_kv_b[occ].reshape(MLA_KV_RANK, MLA_HEADS, MLA_NOPE + MLA_V)
            wkv = jnp.concatenate([wkv[..., :MLA_NOPE].reshape(MLA_KV_RANK, -1),
                                   wkv[..., MLA_NOPE:].reshape(MLA_KV_RANK, -1)], axis=1)
            q = _norm_matmul(u, mla_q_a_norm[occ], bf(wq.reshape(MLA_Q_RANK, -1)), col_block=0, name="mla_q_proj")
            kv = _norm_matmul(u, mla_kv_a_norm[occ], bf(wkv), col_block=1, name="mla_kv_proj")
            pad = jnp.zeros((MLA_QPAD - MLA_QK,), F32)
            gq = jnp.concatenate([mla_q_norm[occ], pad]).reshape(1, MLA_QPAD)
            gk = jnp.concatenate([mla_k_norm[occ], pad]).reshape(1, MLA_QPAD)
            cos_t, sin_t = _rope_tables(s)
            mix = _mla_attention(q, kv, u, krope_block, cos_t, sin_t, gq, gk, b)
            w_out = mla_w_out[occ]
        elif kind == 2:
            u = _norm_matmul(xt, norm_mix[i], bf(swa_w_in[occ]), name="swa_in_proj")
            memq_block = (SWA_Q_HEADS + 2 * SWA_KV_HEADS) * SWA_HEAD_DIM // MEM_WIDTH
            mix = _swa_attention(u, swa_q_norm[occ], swa_k_norm[occ], swa_sinks[occ], rel_bias, b)
            w_out = swa_w_out[occ]
        else:
            w_in = fox_w_in[occ]
            qkv = 3 * FOX_WIDTH
            w_main = jnp.concatenate([w_in[:, :qkv], w_in[:, qkv + FOX_HEADS:]], axis=1)
            w_gate = jnp.concatenate([w_in[:, qkv:qkv + FOX_HEADS], jnp.zeros((d, LANES - FOX_HEADS), F32)], axis=1)
            u = _norm_matmul(xt, norm_mix[i], bf(w_main), name="fox_in_proj")
            u_f = _norm_matmul(xt, norm_mix[i], bf(w_gate), out_dtype=F32, name="fox_gate_proj")
            b_f = jnp.concatenate([fox_b_f[occ], jnp.zeros((LANES - FOX_HEADS,), F32)]).reshape(1, LANES)
            c_gate = _fox_gate_cumsum(u_f, b_f, b)
            memq_block = qkv // MEM_WIDTH
            mix = _fox_attention(u, c_gate, fox_q_norm[occ], fox_k_norm[occ], b)
            w_out = fox_w_out[occ]

        mkv = _norm_matmul(mem2, norm_mem[i], bf(mem_w_kv[i]), name="mem_kv_proj")
        mem_out = _mem_attn(u, memq_block, mkv, mem_q_norm[i], mem_k_norm[i], b)
        k_mix = mix.shape[1]
        xt = _out_proj(mix, mem_out, bf(w_out[:k_mix]), bf(w_out[k_mix:]), xt)

        xt = _ffn(xt, norm_ffn2[i], bf(ffn2_w_gate[i]), bf(ffn2_w_up[i]), bf(ffn2_w_down[i]))
    return xt.reshape(b, s, d)
```

```python
import functools
import math

import numpy as np
import jax
import jax.numpy as jnp
from jax import lax
from jax.experimental import pallas as pl
from jax.experimental.pallas import tpu as pltpu

F32 = jnp.float32
BF16 = jnp.bfloat16

RMS_EPS = 1e-6
NEG_BIG = -1e30
LOG2_E = math.log2(math.e)
LANES = 128
HALF_LANES = LANES // 2
VMEM_LIMIT_BYTES = 56 * 1024 * 1024

BLOCK = 128
MEM_HEADS = 4
MEM_HEAD_DIM = 128
MEM_WIDTH = MEM_HEADS * MEM_HEAD_DIM
CONV_K = 3
MLA_HEADS = 16
MLA_Q_RANK = 512
MLA_KV_RANK = 512
MLA_NOPE = 128
MLA_ROPE = 64
MLA_V = 128
MLA_QK = MLA_NOPE + MLA_ROPE
MLA_QPAD = 2 * LANES
ROPE_THETA = 10000.0
SWA_Q_HEADS = 32
SWA_KV_HEADS = 4
SWA_GROUP = SWA_Q_HEADS // SWA_KV_HEADS
SWA_HEAD_DIM = 64
WINDOW = 128
REL_BUCKETS = 32
REL_MAX_DIST = 128
FOX_HEADS = 32
FOX_HEAD_DIM = 64
FOX_WIDTH = FOX_HEADS * FOX_HEAD_DIM


def _params(*semantics):
    return pltpu.CompilerParams(dimension_semantics=semantics, vmem_limit_bytes=VMEM_LIMIT_BYTES)


def _pick_tile(n, target):
    best = None
    for t in range(LANES, min(n, target) + 1, LANES):
        if n % t == 0:
            best = t
    return n if best is None else best


def _rms_rows(x, gain):
    r = lax.rsqrt(jnp.mean(x * x, axis=-1, keepdims=True) + RMS_EPS)
    return x * r * gain


def _rms_half_lanes(x, gain):
    lane = lax.broadcasted_iota(jnp.int32, x.shape, 1)
    low = lane < HALF_LANES
    sq = x * x
    s_low = jnp.sum(jnp.where(low, sq, 0.0), axis=-1, keepdims=True)
    s_high = jnp.sum(jnp.where(low, 0.0, sq), axis=-1, keepdims=True)
    r = lax.rsqrt(jnp.where(low, s_low, s_high) * (1.0 / HALF_LANES) + RMS_EPS)
    return x * r * gain


def _dot(a, b):
    return jnp.dot(a, b, preferred_element_type=F32)


def _dot_nt(a, b):
    return lax.dot_general(a, b, (((1,), (1,)), ((), ())), preferred_element_type=F32)


FFN_ROWS = 1024
FFN_SPLIT = 2


def _ffn_kernel(x_ref, g_ref, wg_ref, wu_ref, wd_ref, o_ref, h_scr):
    @pl.when(pl.program_id(1) == 0)
    def _():
        x = x_ref[...]
        h_scr[...] = _rms_rows(x, g_ref[...]).astype(BF16)
        o_ref[...] = x

    h = h_scr[...]
    width = wg_ref.shape[-1] // FFN_SPLIT
    acts = []
    for c in range(FFN_SPLIT):
        cols = slice(c * width, (c + 1) * width)
        gate = _dot(h, wg_ref[:, cols])
        up = _dot(h, wu_ref[:, cols])
        acts.append((gate / (1.0 + jnp.exp(-gate)) * (0.5 * up)).astype(BF16))
    o_ref[...] += _dot(jnp.concatenate(acts, axis=-1), wd_ref[...])


def _ffn(x, gain, w_gate, w_up, w_down, layer):
    t, d = x.shape
    f = w_gate.shape[-1]
    tm = _pick_tile(t, FFN_ROWS)
    tf = _pick_tile(f, 512)
    return pl.pallas_call(
        _ffn_kernel,
        grid=(t // tm, f // tf),
        in_specs=[
            pl.BlockSpec((tm, d), lambda i, j: (i, 0)),
            pl.BlockSpec((None, 1, d), lambda i, j: (layer, 0, 0)),
            pl.BlockSpec((None, d, tf), lambda i, j: (layer, 0, j)),
            pl.BlockSpec((None, d, tf), lambda i, j: (layer, 0, j)),
            pl.BlockSpec((None, tf, d), lambda i, j: (layer, j, 0)),
        ],
        out_specs=pl.BlockSpec((tm, d), lambda i, j: (i, 0)),
        out_shape=jax.ShapeDtypeStruct((t, d), F32),
        scratch_shapes=[pltpu.VMEM((tm, d), BF16)],
        compiler_params=_params("parallel", "arbitrary"),
        name="ffn",
    )(x, gain.reshape(gain.shape[0], 1, d), w_gate, w_up, w_down)


def _norm_matmul_kernel(x_ref, g_ref, w_ref, o_ref, h_scr):
    @pl.when(pl.program_id(1) == 0)
    def _():
        h_scr[...] = _rms_rows(x_ref[...].astype(F32), g_ref[...]).astype(BF16)

    o_ref[...] = _dot(h_scr[...], w_ref[...]).astype(o_ref.dtype)


def _norm_matmul(x, gain, w, *, col_block=0, out_dtype=BF16, name="norm_matmul"):
    m = x.shape[0]
    k, n = w.shape
    tm = _pick_tile(m, 1024)
    tn = n if n <= 2048 else _pick_tile(n, 1024)
    return pl.pallas_call(
        _norm_matmul_kernel,
        grid=(m // tm, n // tn),
        in_specs=[
            pl.BlockSpec((tm, k), lambda i, j: (i, col_block)),
            pl.BlockSpec((1, k), lambda i, j: (0, 0)),
            pl.BlockSpec((k, tn), lambda i, j: (0, j)),
        ],
        out_specs=pl.BlockSpec((tm, tn), lambda i, j: (i, j)),
        out_shape=jax.ShapeDtypeStruct((m, n), out_dtype),
        scratch_shapes=[pltpu.VMEM((tm, k), BF16)],
        compiler_params=_params("parallel", "arbitrary"),
        name=name,
    )(x, gain.reshape(1, k), w)


def _out_proj_kernel(a_ref, b_ref, wa_ref, wb_ref, x_ref, o_ref):
    acc = _dot(a_ref[...], wa_ref[...])
    acc = acc + _dot(b_ref[...], wb_ref[...])
    o_ref[...] = x_ref[...] + acc


def _out_proj(mix, mem_out, w_mix, w_mem, x):
    t, d = x.shape
    ka = mix.shape[1]
    kb = mem_out.shape[1]
    tm = _pick_tile(t, 512)
    tn = _pick_tile(d, 2048)
    return pl.pallas_call(
        _out_proj_kernel,
        grid=(t // tm, d // tn),
        in_specs=[
            pl.BlockSpec((tm, ka), lambda i, j: (i, 0)),
            pl.BlockSpec((tm, kb), lambda i, j: (i, 0)),
            pl.BlockSpec((ka, tn), lambda i, j: (0, j)),
            pl.BlockSpec((kb, tn), lambda i, j: (0, j)),
            pl.BlockSpec((tm, tn), lambda i, j: (i, j)),
        ],
        out_specs=pl.BlockSpec((tm, tn), lambda i, j: (i, j)),
        out_shape=jax.ShapeDtypeStruct((t, d), F32),
        compiler_params=_params("parallel", "parallel"),
        name="out_proj",
    )(mix, mem_out, w_mix, w_mem, x)


def _mem_attn_kernel(q_ref, kv_ref, gq_ref, gk_ref, o_ref):
    scale = 1.0 / math.sqrt(MEM_HEAD_DIM)
    for h in range(MEM_HEADS):
        cols = slice(h * MEM_HEAD_DIM, (h + 1) * MEM_HEAD_DIM)
        vcols = slice(MEM_WIDTH + h * MEM_HEAD_DIM, MEM_WIDTH + (h + 1) * MEM_HEAD_DIM)
        q = (_rms_rows(q_ref[:, cols].astype(F32), gq_ref[...]) * scale).astype(BF16)
        k = _rms_rows(kv_ref[:, cols].astype(F32), gk_ref[...]).astype(BF16)
        s = _dot_nt(q, k)
        m = jnp.max(s, axis=-1, keepdims=True)
        p = jnp.exp(s - m)
        l = jnp.sum(p, axis=-1, keepdims=True)
        o = _dot(p.astype(BF16), kv_ref[:, vcols]) / l
        o_ref[:, cols] = o.astype(o_ref.dtype)


def _mem_attn(u, q_col_block, mkv, gq, gk, batch):
    t = u.shape[0]
    s = t // batch
    m_len = mkv.shape[0] // batch
    tq = _pick_tile(s, 512)
    nq = s // tq
    return pl.pallas_call(
        _mem_attn_kernel,
        grid=(batch, nq),
        in_specs=[
            pl.BlockSpec((tq, MEM_WIDTH), lambda b, i: (b * nq + i, q_col_block)),
            pl.BlockSpec((m_len, 2 * MEM_WIDTH), lambda b, i: (b, 0)),
            pl.BlockSpec((1, MEM_HEAD_DIM), lambda b, i: (0, 0)),
            pl.BlockSpec((1, MEM_HEAD_DIM), lambda b, i: (0, 0)),
        ],
        out_specs=pl.BlockSpec((tq, MEM_WIDTH), lambda b, i: (b * nq + i, 0)),
        out_shape=jax.ShapeDtypeStruct((t, MEM_WIDTH), BF16),
        compiler_params=_params("parallel", "parallel"),
        name="mem_attn",
    )(u, mkv, gq.reshape(1, MEM_HEAD_DIM), gk.reshape(1, MEM_HEAD_DIM))


CONV_HALO = 16


def _conv_kernel(gb_ref, gc_ref, xt_ref, gch_ref, xth_ref, w_ref, o_ref, *, tiles_per_seq):
    i = pl.program_id(0)
    z = gc_ref[...].astype(F32) * xt_ref[...].astype(F32)
    zh = gch_ref[...].astype(F32) * xth_ref[...].astype(F32)
    zh = jnp.where(i % tiles_per_seq == 0, 0.0, zh)
    zc = jnp.concatenate([zh[CONV_HALO - 8:], z], axis=0)
    z1 = pltpu.roll(zc, 1, 0)[8:]
    z2 = pltpu.roll(zc, 2, 0)[8:]
    w = w_ref[...]
    conv = z2 * w[0:1] + z1 * w[1:2] + z * w[2:3]
    o_ref[...] = (gb_ref[...].astype(F32) * conv).astype(o_ref.dtype)


def _conv_mixer(u, conv_w, batch):
    t = u.shape[0]
    s = t // batch
    c = conv_w.shape[1]
    ts = _pick_tile(s, 512)
    tc = _pick_tile(c, 512)
    nc = c // tc
    halo_per_tile = ts // CONV_HALO

    def halo_map(col0):
        return lambda i, j: (jnp.maximum(i * halo_per_tile - 1, 0), col0 + j)

    return pl.pallas_call(
        functools.partial(_conv_kernel, tiles_per_seq=s // ts),
        grid=(t // ts, nc),
        in_specs=[
            pl.BlockSpec((ts, tc), lambda i, j: (i, j)),
            pl.BlockSpec((ts, tc), lambda i, j: (i, nc + j)),
            pl.BlockSpec((ts, tc), lambda i, j: (i, 2 * nc + j)),
            pl.BlockSpec((CONV_HALO, tc), halo_map(nc)),
            pl.BlockSpec((CONV_HALO, tc), halo_map(2 * nc)),
            pl.BlockSpec((CONV_K, tc), lambda i, j: (0, j)),
        ],
        out_specs=pl.BlockSpec((ts, tc), lambda i, j: (i, j)),
        out_shape=jax.ShapeDtypeStruct((t, c), BF16),
        compiler_params=_params("parallel", "parallel"),
        name="conv_mixer",
    )(u, u, u, u, u, conv_w)


def _online_softmax_step(s, v, m_ref, l_ref, acc_ref):
    m_prev = m_ref[...]
    m_new = jnp.maximum(m_prev, jnp.max(s, axis=-1, keepdims=True))
    alpha = jnp.exp2(m_prev - m_new)
    p = [jnp.exp2(s[:, t * LANES:(t + 1) * LANES] - m_new) for t in range(s.shape[1] // LANES)]
    l_ref[...] = alpha * l_ref[...] + functools.reduce(lambda a, b: a + b, p)
    acc_ref[...] = alpha * acc_ref[...] + _dot(jnp.concatenate(p, axis=-1).astype(BF16), v)
    m_ref[...] = m_new


def _softmax_finish(l_ref, acc_ref):
    return acc_ref[...] / jnp.sum(l_ref[...], axis=-1, keepdims=True)


def _causal_mask(s):
    row = lax.broadcasted_iota(jnp.int32, s.shape, 0)
    col = lax.broadcasted_iota(jnp.int32, s.shape, 1)
    return jnp.where(col <= row, s, NEG_BIG)


def _swap_rope_halves(t):
    lane = lax.broadcasted_iota(jnp.int32, t.shape, 1)
    up = pltpu.roll(t, MLA_ROPE // 2, 1)
    down = pltpu.roll(t, LANES - MLA_ROPE // 2, 1)
    return jnp.where(lane < MLA_ROPE // 2, down, up)


def _rope_lanes(t, cos_t, sin_t):
    return t * cos_t + _swap_rope_halves(t) * sin_t


def _rms_rope_part(x, gain):
    r = lax.rsqrt(jnp.sum(x * x, axis=-1, keepdims=True) * (1.0 / MLA_ROPE) + RMS_EPS)
    return x * r * gain


MLA_HEADS_PER_STEP = 2


def _mla_attn_kernel(q_ref, kn_ref, v_ref, kr_ref, cosq_ref, sinq_ref, cosk_ref, sink_ref,
                     gq_ref, gk_ref, o_ref, k_scr, *state_scr, tq):
    i = pl.program_id(2)
    state = [state_scr[3 * e:3 * e + 3] for e in range(MLA_HEADS_PER_STEP)]

    @pl.when(i == 0)
    def _():
        kr = _rms_rope_part(kr_ref[...].astype(F32), gk_ref[:, LANES:])
        kr = _rope_lanes(kr, cosk_ref[...], sink_ref[...]).astype(BF16)
        for e in range(MLA_HEADS_PER_STEP):
            kn = kn_ref[:, e * LANES:(e + 1) * LANES].astype(F32)
            k_scr[e, :, :LANES] = _rms_rows(kn, gk_ref[:, :LANES]).astype(BF16)
            k_scr[e, :, LANES:] = kr

    scale = LOG2_E / math.sqrt(MLA_QK)
    q_heads = []
    for e in range(MLA_HEADS_PER_STEP):
        c0 = e * MLA_QPAD
        q_nope = _rms_rows(q_ref[:, c0:c0 + LANES].astype(F32), gq_ref[:, :LANES])
        q_rot = _rms_rope_part(q_ref[:, c0 + LANES:c0 + MLA_QPAD].astype(F32), gq_ref[:, LANES:])
        q_rot = _rope_lanes(q_rot, cosq_ref[...], sinq_ref[...])
        q_heads.append((jnp.concatenate([q_nope, q_rot], axis=-1) * scale).astype(BF16))

    for m_scr, l_scr, acc_scr in state:
        m_scr[...] = jnp.full(m_scr.shape, NEG_BIG, F32)
        l_scr[...] = jnp.zeros(l_scr.shape, F32)
        acc_scr[...] = jnp.zeros(acc_scr.shape, F32)

    def chunk(j, masked):
        rows = pl.ds(pl.multiple_of(j * tq, tq), tq)
        for e in range(MLA_HEADS_PER_STEP):
            s = _dot_nt(q_heads[e], k_scr[e, rows, :])
            if masked:
                s = _causal_mask(s)
            _online_softmax_step(s, v_ref[rows, e * MLA_V:(e + 1) * MLA_V], *state[e])

    def body(j, carry):
        chunk(j, False)
        return carry

    lax.fori_loop(0, i, body, 0)
    chunk(i, True)
    for e, (_, l_scr, acc_scr) in enumerate(state):
        o_ref[:, e * MLA_V:(e + 1) * MLA_V] = _softmax_finish(l_scr, acc_scr).astype(o_ref.dtype)


def _mla_attention(q, kv, u, krope_block, cos_t, sin_t, gq, gk, batch):
    t = q.shape[0]
    s = t // batch
    tq = 512 if s % 512 == 0 and s > 512 else BLOCK
    nq = s // tq
    h = MLA_HEADS
    hs = MLA_HEADS_PER_STEP
    groups = h // hs
    kernel = functools.partial(_mla_attn_kernel, tq=tq)
    return pl.pallas_call(
        kernel,
        grid=(batch, groups, nq),
        in_specs=[
            pl.BlockSpec((tq, hs * MLA_QPAD), lambda b, hh, i: (b * nq + i, hh)),
            pl.BlockSpec((s, hs * LANES), lambda b, hh, i: (b, hh)),
            pl.BlockSpec((s, hs * LANES), lambda b, hh, i: (b, groups + hh)),
            pl.BlockSpec((s, LANES), lambda b, hh, i: (b, krope_block)),
            pl.BlockSpec((tq, LANES), lambda b, hh, i: (i, 0)),
            pl.BlockSpec((tq, LANES), lambda b, hh, i: (i, 0)),
            pl.BlockSpec((s, LANES), lambda b, hh, i: (0, 0)),
            pl.BlockSpec((s, LANES), lambda b, hh, i: (0, 0)),
            pl.BlockSpec((1, MLA_QPAD), lambda b, hh, i: (0, 0)),
            pl.BlockSpec((1, MLA_QPAD), lambda b, hh, i: (0, 0)),
        ],
        out_specs=pl.BlockSpec((tq, hs * MLA_V), lambda b, hh, i: (b * nq + i, hh)),
        out_shape=jax.ShapeDtypeStruct((t, h * MLA_V), BF16),
        scratch_shapes=[pltpu.VMEM((hs, s, MLA_QPAD), BF16)] + hs * [
            pltpu.VMEM((tq, LANES), F32),
            pltpu.VMEM((tq, LANES), F32),
            pltpu.VMEM((tq, MLA_V), F32),
        ],
        compiler_params=_params("parallel", "parallel", "arbitrary"),
        name="mla_attention",
    )(q, kv, kv, u, cos_t, sin_t, cos_t, sin_t, gq, gk)


def _rope_tables(s):
    half = MLA_ROPE // 2
    inv = ROPE_THETA ** (-jnp.arange(half, dtype=F32) / half)
    ang = jnp.arange(s).astype(F32)[:, None] * inv
    cos, sin = jnp.cos(ang), jnp.sin(ang)
    zeros = jnp.zeros((s, LANES - MLA_ROPE), F32)
    return (jnp.concatenate([cos, cos, zeros], axis=-1), jnp.concatenate([-sin, sin, zeros], axis=-1))


SWA_CHUNKS = SWA_GROUP * SWA_HEAD_DIM // LANES


def _swa_kernel(sinks_ref, q_ref, kp_ref, kc_ref, vp_ref, vc_ref, bias_ref, gq_ref, gk_ref, o_ref):
    kh = pl.program_id(1)
    n = pl.program_id(2)
    kv_half = kh % 2
    kv_is_low = kv_half == 0
    lane = lax.broadcasted_iota(jnp.int32, (BLOCK, LANES), 1)
    own = lane // HALF_LANES == kv_half
    scale = 1.0 / math.sqrt(SWA_HEAD_DIM)

    q = jnp.concatenate(
        [_rms_half_lanes(q_ref[:, c * LANES:(c + 1) * LANES].astype(F32), gq_ref[...]) * scale
         for c in range(SWA_CHUNKS)], axis=0).astype(BF16)

    def by_parity(x):
        mine = jnp.where(own, x, 0.0)
        other = pltpu.roll(mine, HALF_LANES, 1)
        return (jnp.where(kv_is_low, mine, other).astype(BF16), jnp.where(kv_is_low, other, mine).astype(BF16))

    kp = by_parity(_rms_half_lanes(kp_ref[...].astype(F32), gk_ref[...]))
    kc = by_parity(_rms_half_lanes(kc_ref[...].astype(F32), gk_ref[...]))
    vp = by_parity(vp_ref[...].astype(F32))
    vc = by_parity(vc_ref[...].astype(F32))

    row_chunk = lax.broadcasted_iota(jnp.int32, (SWA_CHUNKS * BLOCK, 1), 0) // BLOCK
    out = None
    for e in range(2):
        sink = jnp.zeros((SWA_CHUNKS * BLOCK, 1), F32)
        for c in range(SWA_CHUNKS):
            sink = jnp.where(row_chunk == c, sinks_ref[kh * SWA_GROUP + 2 * c + e], sink)
        s_prev = jnp.where(n > 0, _dot_nt(q, kp[e]) + bias_ref[0, e, 0], NEG_BIG)
        s_cur = _dot_nt(q, kc[e]) + bias_ref[0, e, 1]
        m = jnp.maximum(jnp.maximum(jnp.max(s_prev, axis=-1, keepdims=True),
                                    jnp.max(s_cur, axis=-1, keepdims=True)), sink)
        p_prev = jnp.exp(s_prev - m)
        p_cur = jnp.exp(s_cur - m)
        l = (jnp.sum(p_prev, axis=-1, keepdims=True) + jnp.sum(p_cur, axis=-1, keepdims=True)
             + jnp.exp(sink - m))
        o = (_dot(p_prev.astype(BF16), vp[e]) + _dot(p_cur.astype(BF16), vc[e])) / l
        out = o if out is None else out + o
    for c in range(SWA_CHUNKS):
        o_ref[:, c * LANES:(c + 1) * LANES] = out[c * BLOCK:(c + 1) * BLOCK].astype(o_ref.dtype)


def _t5_causal_bucket(dist):
    exact = REL_BUCKETS // 2
    d = np.maximum(dist, 0)
    log_b = exact + (np.log(np.maximum(d, 1) / exact) / np.log(REL_MAX_DIST / exact)
                     * (REL_BUCKETS - exact)).astype(np.int32)
    log_b = np.minimum(log_b, REL_BUCKETS - 1)
    return np.where(d < exact, d, log_b).astype(np.int32)


def _swa_bias_table(rel_bias):
    hq = rel_bias.shape[1]
    by_dist = rel_bias.astype(F32)[_t5_causal_bucket(np.arange(WINDOW))]
    masked = lambda width: jnp.full((hq, width), NEG_BIG, F32)
    n = 3 * BLOCK - 1
    w = jnp.concatenate([masked(BLOCK), by_dist[::-1].T, masked(BLOCK - 1)], axis=1)
    u = jnp.roll(w, -(BLOCK - 1), axis=1)
    flat = jnp.broadcast_to(u[:, None, :], (hq, BLOCK, n)).reshape(hq, BLOCK * n)
    bias = flat[:, :BLOCK * (n - 1)].reshape(hq, BLOCK, n - 1)[:, :, :2 * BLOCK]
    bias = bias.reshape(SWA_KV_HEADS, SWA_CHUNKS, 2, BLOCK, 2, BLOCK)
    bias = bias.transpose(0, 2, 4, 1, 3, 5)
    return bias.reshape(SWA_KV_HEADS, 2, 2, SWA_CHUNKS * BLOCK, BLOCK)


def _swa_attention(u, q_norm, k_norm, sinks, rel_bias, batch):
    t = u.shape[0]
    s = t // batch
    nblk = s // BLOCK
    nq_cols = SWA_Q_HEADS * SWA_HEAD_DIM
    k_block0 = nq_cols // LANES
    v_block0 = (nq_cols + SWA_KV_HEADS * SWA_HEAD_DIM) // LANES
    group_cols = SWA_CHUNKS * LANES
    gq = jnp.tile(q_norm.reshape(1, SWA_HEAD_DIM), (1, 2))
    gk = jnp.tile(k_norm.reshape(1, SWA_HEAD_DIM), (1, 2))

    def prev_map(col0):
        return lambda b, kh, n: (b * nblk + jnp.maximum(n - 1, 0), col0 + kh // 2)

    def cur_map(col0):
        return lambda b, kh, n: (b * nblk + n, col0 + kh // 2)

    return pl.pallas_call(
        _swa_kernel,
        grid=(batch, SWA_KV_HEADS, nblk),
        in_specs=[
            pl.BlockSpec(memory_space=pltpu.SMEM),
            pl.BlockSpec((BLOCK, group_cols), lambda b, kh, n: (b * nblk + n, kh)),
            pl.BlockSpec((BLOCK, LANES), prev_map(k_block0)),
            pl.BlockSpec((BLOCK, LANES), cur_map(k_block0)),
            pl.BlockSpec((BLOCK, LANES), prev_map(v_block0)),
            pl.BlockSpec((BLOCK, LANES), cur_map(v_block0)),
            pl.BlockSpec((1, 2, 2, SWA_CHUNKS * BLOCK, BLOCK), lambda b, kh, n: (kh, 0, 0, 0, 0)),
            pl.BlockSpec((1, LANES), lambda b, kh, n: (0, 0)),
            pl.BlockSpec((1, LANES), lambda b, kh, n: (0, 0)),
        ],
        out_specs=pl.BlockSpec((BLOCK, group_cols), lambda b, kh, n: (b * nblk + n, kh)),
        out_shape=jax.ShapeDtypeStruct((t, nq_cols), BF16),
        compiler_params=_params("parallel", "parallel", "parallel"),
        name="swa_attention",
    )(sinks.astype(F32), u, u, u, u, u, _swa_bias_table(rel_bias), gq, gk)


def _fox_gate_kernel(u_ref, b_ref, o_ref):
    x = u_ref[...] + b_ref[...]
    log_f = -(jnp.maximum(-x, 0.0) + jnp.log1p(jnp.exp(-jnp.abs(x))))
    row = lax.broadcasted_iota(jnp.int32, log_f.shape, 0)
    c = log_f
    shift = 1
    while shift < c.shape[0]:
        c = c + jnp.where(row >= shift, pltpu.roll(c, shift, 0), 0.0)
        shift *= 2
    o_ref[...] = c


def _fox_gate_cumsum(u_f, b_f, batch):
    t = u_f.shape[0]
    s = t // batch
    return pl.pallas_call(
        _fox_gate_kernel,
        grid=(batch,),
        in_specs=[pl.BlockSpec((s, LANES), lambda b: (b, 0)), pl.BlockSpec((1, LANES), lambda b: (0, 0))],
        out_specs=pl.BlockSpec((s, LANES), lambda b: (b, 0)),
        out_shape=jax.ShapeDtypeStruct((t, LANES), F32),
        compiler_params=_params("parallel"),
        name="fox_gate_cumsum",
    )(u_f, b_f)


FOX_GATE_PIECES = 3


def _split_bf16_pieces(x):
    pieces = []
    for _ in range(FOX_GATE_PIECES - 1):
        piece = x.astype(BF16).astype(F32)
        pieces.append(piece)
        x = x - piece
    return pieces + [x]


def _fox_operand(x, gate_col, lane, head, is_query):
    base = (1 - head) * HALF_LANES
    out = jnp.where(lane // HALF_LANES == head, x, 0.0)
    for t, piece in enumerate(_split_bf16_pieces(gate_col)):
        gate_lane, one_lane = base + t, base + FOX_GATE_PIECES + t
        if not is_query:
            gate_lane, one_lane, piece = one_lane, gate_lane, -piece
        out = jnp.where(lane == gate_lane, piece, out)
        out = jnp.where(lane == one_lane, 1.0, out)
    return out.astype(BF16)


def _fox_attn_kernel(q_ref, k_ref, v_ref, cq_ref, ck_ref, gq_ref, gk_ref, o_ref,
                     k_scr, m0_scr, l0_scr, acc0_scr, m1_scr, l1_scr, acc1_scr, *, tq):
    hp = pl.program_id(1)
    i = pl.program_id(2)

    def gate_col(c, e):
        lane = lax.broadcasted_iota(jnp.int32, c.shape, 1)
        return jnp.sum(jnp.where(lane == 2 * hp + e, c, 0.0), axis=-1, keepdims=True) * LOG2_E

    @pl.when(i == 0)
    def _():
        kn = _rms_half_lanes(k_ref[...].astype(F32), gk_ref[...])
        lane = lax.broadcasted_iota(jnp.int32, kn.shape, 1)
        for e in range(2):
            k_scr[e] = _fox_operand(kn, gate_col(ck_ref[...], e), lane, e, False)

    scale = LOG2_E / math.sqrt(FOX_HEAD_DIM)
    lane = lax.broadcasted_iota(jnp.int32, (tq, LANES), 1)
    qn = _rms_half_lanes(q_ref[...].astype(F32), gq_ref[...]) * scale
    q_heads = [_fox_operand(qn, gate_col(cq_ref[...], e), lane, e, True) for e in range(2)]

    state = ((m0_scr, l0_scr, acc0_scr), (m1_scr, l1_scr, acc1_scr))
    for m_scr, l_scr, acc_scr in state:
        m_scr[...] = jnp.full(m_scr.shape, NEG_BIG, F32)
        l_scr[...] = jnp.zeros(l_scr.shape, F32)
        acc_scr[...] = jnp.zeros(acc_scr.shape, F32)

    def chunk(j, masked):
        rows = pl.ds(pl.multiple_of(j * tq, tq), tq)
        v = v_ref[rows, :]
        for e in range(2):
            s = _dot_nt(q_heads[e], k_scr[e, rows, :])
            if masked:
                s = _causal_mask(s)
            _online_softmax_step(s, v, *state[e])

    def body(j, carry):
        chunk(j, False)
        return carry

    lax.fori_loop(0, i, body, 0)
    chunk(i, True)
    o_ref[...] = jnp.where(lane < HALF_LANES, _softmax_finish(l0_scr, acc0_scr),
                           _softmax_finish(l1_scr, acc1_scr)).astype(o_ref.dtype)


def _fox_attention(u, c, q_norm, k_norm, batch):
    t = u.shape[0]
    s = t // batch
    tq = 512 if s % 512 == 0 and s > 512 else BLOCK
    nq = s // tq
    pairs = FOX_HEADS // 2
    blocks = FOX_WIDTH // LANES
    gq = jnp.tile(q_norm.reshape(1, FOX_HEAD_DIM), (1, 2))
    gk = jnp.tile(k_norm.reshape(1, FOX_HEAD_DIM), (1, 2))
    kernel = functools.partial(_fox_attn_kernel, tq=tq)
    return pl.pallas_call(
        kernel,
        grid=(batch, pairs, nq),
        in_specs=[
            pl.BlockSpec((tq, LANES), lambda b, hp, i: (b * nq + i, hp)),
            pl.BlockSpec((s, LANES), lambda b, hp, i: (b, blocks + hp)),
            pl.BlockSpec((s, LANES), lambda b, hp, i: (b, 2 * blocks + hp)),
            pl.BlockSpec((tq, LANES), lambda b, hp, i: (b * nq + i, 0)),
            pl.BlockSpec((s, LANES), lambda b, hp, i: (b, 0)),
            pl.BlockSpec((1, LANES), lambda b, hp, i: (0, 0)),
            pl.BlockSpec((1, LANES), lambda b, hp, i: (0, 0)),
        ],
        out_specs=pl.BlockSpec((tq, LANES), lambda b, hp, i: (b * nq + i, hp)),
        out_shape=jax.ShapeDtypeStruct((t, FOX_WIDTH), BF16),
        scratch_shapes=[pltpu.VMEM((2, s, LANES), BF16)] + 2 * [
            pltpu.VMEM((tq, LANES), F32),
            pltpu.VMEM((tq, LANES), F32),
            pltpu.VMEM((tq, LANES), F32),
        ],
        compiler_params=_params("parallel", "parallel", "arbitrary"),
        name="fox_attention",
    )(u, u, u, c, c, gq, gk)


def kernel(x, mem, norm_ffn1, ffn1_w_gate, ffn1_w_up, ffn1_w_down, norm_mix, norm_ffn2, ffn2_w_gate, ffn2_w_up, ffn2_w_down, norm_mem, mem_w_kv, mem_q_norm, mem_k_norm, conv_w_in, conv_w, conv_w_out, mla_w_in, mla_q_a_norm, mla_w_q_b, mla_kv_a_norm, mla_w_kv_b, mla_q_norm, mla_k_norm, mla_w_out, swa_w_in, swa_q_norm, swa_k_norm, swa_sinks, swa_w_out, rel_bias, fox_w_in, fox_b_f, fox_q_norm, fox_k_norm, fox_w_out):
    b, s, d = x.shape
    m_len = mem.shape[1]
    depth = norm_ffn1.shape[0]
    xt = x.reshape(b * s, d)
    mem2 = mem.reshape(b * m_len, d)
    bf = lambda w: w.astype(BF16)
    ffn1_w = (bf(ffn1_w_gate), bf(ffn1_w_up), bf(ffn1_w_down))
    ffn2_w = (bf(ffn2_w_gate), bf(ffn2_w_up), bf(ffn2_w_down))

    for i in range(depth):
        kind, occ = i % 4, i // 4
        xt = _ffn(xt, norm_ffn1, *ffn1_w, i)

        if kind == 0:
            c = conv_w.shape[-1]
            u = _norm_matmul(xt, norm_mix[i], bf(conv_w_in[occ]), name="conv_in_proj")
            memq_block = 3 * c // MEM_WIDTH
            mix = _conv_mixer(u, conv_w[occ], b)
            w_out = conv_w_out[occ]
        elif kind == 1:
            w_in = mla_w_in[occ]
            lat = MLA_Q_RANK + MLA_KV_RANK
            w_in = jnp.concatenate([w_in[:, :lat], w_in[:, lat + MLA_ROPE:], w_in[:, lat:lat + MLA_ROPE],
                                    jnp.zeros((d, LANES - MLA_ROPE), F32)], axis=1)
            u = _norm_matmul(xt, norm_mix[i], bf(w_in), name="mla_in_proj")
            memq_block = lat // MEM_WIDTH
            krope_block = (lat + MEM_WIDTH) // LANES
            wq = mla_w_q_b[occ].reshape(MLA_Q_RANK, MLA_HEADS, MLA_QK)
            wq = jnp.concatenate([wq, jnp.zeros((MLA_Q_RANK, MLA_HEADS, MLA_QPAD - MLA_QK), F32)], axis=-1)
            wkv = mla_w_kv_b[occ].reshape(MLA_KV_RANK, MLA_HEADS, MLA_NOPE + MLA_V)
            wkv = jnp.concatenate([wkv[..., :MLA_NOPE].reshape(MLA_KV_RANK, -1),
                                   wkv[..., MLA_NOPE:].reshape(MLA_KV_RANK, -1)], axis=1)
            q = _norm_matmul(u, mla_q_a_norm[occ], bf(wq.reshape(MLA_Q_RANK, -1)), col_block=0, name="mla_q_proj")
            kv = _norm_matmul(u, mla_kv_a_norm[occ], bf(wkv), col_block=1, name="mla_kv_proj")
            pad = jnp.zeros((MLA_QPAD - MLA_QK,), F32)
            gq = jnp.concatenate([mla_q_norm[occ], pad]).reshape(1, MLA_QPAD)
            gk = jnp.concatenate([mla_k_norm[occ], pad]).reshape(1, MLA_QPAD)
            cos_t, sin_t = _rope_tables(s)
            mix = _mla_attention(q, kv, u, krope_block, cos_t, sin_t, gq, gk, b)
            w_out = mla_w_out[occ]
        elif kind == 2:
            u = _norm_matmul(xt, norm_mix[i], bf(swa_w_in[occ]), name="swa_in_proj")
            memq_block = (SWA_Q_HEADS + 2 * SWA_KV_HEADS) * SWA_HEAD_DIM // MEM_WIDTH
            mix = _swa_attention(u, swa_q_norm[occ], swa_k_norm[occ], swa_sinks[occ], rel_bias, b)
            w_out = swa_w_out[occ]
        else:
            w_in = fox_w_in[occ]
            qkv = 3 * FOX_WIDTH
            w_main = jnp.concatenate([w_in[:, :qkv], w_in[:, qkv + FOX_HEADS:]], axis=1)
            w_gate = jnp.concatenate([w_in[:, qkv:qkv + FOX_HEADS], jnp.zeros((d, LANES - FOX_HEADS), F32)], axis=1)
            u = _norm_matmul(xt, norm_mix[i], bf(w_main), name="fox_in_proj")
            u_f = _norm_matmul(xt, norm_mix[i], bf(w_gate), out_dtype=F32, name="fox_gate_proj")
            b_f = jnp.concatenate([fox_b_f[occ], jnp.zeros((LANES - FOX_HEADS,), F32)]).reshape(1, LANES)
            c_gate = _fox_gate_cumsum(u_f, b_f, b)
            memq_block = qkv // MEM_WIDTH
            mix = _fox_attention(u, c_gate, fox_q_norm[occ], fox_k_norm[occ], b)
            w_out = fox_w_out[occ]

        mkv = _norm_matmul(mem2, norm_mem[i], bf(mem_w_kv[i]), name="mem_kv_proj")
        mem_out = _mem_attn(u, memq_block, mkv, mem_q_norm[i], mem_k_norm[i], b)
        k_mix = mix.shape[1]
        xt = _out_proj(mix, mem_out, bf(w_out[:k_mix]), bf(w_out[k_mix:]), xt)

        xt = _ffn(xt, norm_ffn2, *ffn2_w, i)
    return xt.reshape(b, s, d)
```

```python
import functools
import math

import numpy as np
import jax
import jax.numpy as jnp
from jax import lax
from jax.experimental import pallas as pl
from jax.experimental.pallas import tpu as pltpu

F32 = jnp.float32
BF16 = jnp.bfloat16

RMS_EPS = 1e-6
NEG_BIG = -1e30
LOG2_E = math.log2(math.e)
LANES = 128
HALF_LANES = LANES // 2
VMEM_LIMIT_BYTES = 56 * 1024 * 1024

BLOCK = 128
MEM_HEADS = 4
MEM_HEAD_DIM = 128
MEM_WIDTH = MEM_HEADS * MEM_HEAD_DIM
CONV_K = 3
MLA_HEADS = 16
MLA_Q_RANK = 512
MLA_KV_RANK = 512
MLA_NOPE = 128
MLA_ROPE = 64
MLA_V = 128
MLA_QK = MLA_NOPE + MLA_ROPE
MLA_QPAD = 2 * LANES
ROPE_THETA = 10000.0
SWA_Q_HEADS = 32
SWA_KV_HEADS = 4
SWA_GROUP = SWA_Q_HEADS // SWA_KV_HEADS
SWA_HEAD_DIM = 64
WINDOW = 128
REL_BUCKETS = 32
REL_MAX_DIST = 128
FOX_HEADS = 32
FOX_HEAD_DIM = 64
FOX_WIDTH = FOX_HEADS * FOX_HEAD_DIM


def _params(*semantics):
    return pltpu.CompilerParams(dimension_semantics=semantics, vmem_limit_bytes=VMEM_LIMIT_BYTES)


def _pick_tile(n, target):
    best = None
    for t in range(LANES, min(n, target) + 1, LANES):
        if n % t == 0:
            best = t
    return n if best is None else best


def _rms_rows(x, gain):
    r = lax.rsqrt(jnp.mean(x * x, axis=-1, keepdims=True) + RMS_EPS)
    return x * r * gain


def _rms_half_lanes(x, gain):
    lane = lax.broadcasted_iota(jnp.int32, x.shape, 1)
    low = lane < HALF_LANES
    sq = x * x
    s_low = jnp.sum(jnp.where(low, sq, 0.0), axis=-1, keepdims=True)
    s_high = jnp.sum(jnp.where(low, 0.0, sq), axis=-1, keepdims=True)
    r = lax.rsqrt(jnp.where(low, s_low, s_high) * (1.0 / HALF_LANES) + RMS_EPS)
    return x * r * gain


def _dot(a, b):
    return jnp.dot(a, b, preferred_element_type=F32)


def _dot_nt(a, b):
    return lax.dot_general(a, b, (((1,), (1,)), ((), ())), preferred_element_type=F32)


FFN_ROWS = 1024
FFN_SPLIT = 2


def _ffn_kernel(x_ref, g_ref, wg_ref, wu_ref, wd_ref, o_ref, h_scr):
    @pl.when(pl.program_id(1) == 0)
    def _():
        x = x_ref[...]
        h_scr[...] = _rms_rows(x, g_ref[...]).astype(BF16)
        o_ref[...] = x

    h = h_scr[...]
    width = wg_ref.shape[-1] // FFN_SPLIT
    acts = []
    for c in range(FFN_SPLIT):
        cols = slice(c * width, (c + 1) * width)
        gate = _dot(h, wg_ref[:, cols])
        up = _dot(h, wu_ref[:, cols])
        acts.append((gate / (1.0 + jnp.exp(-gate)) * (0.5 * up)).astype(BF16))
    o_ref[...] += _dot(jnp.concatenate(acts, axis=-1), wd_ref[...])


def _ffn(x, gain, w_gate, w_up, w_down, layer):
    t, d = x.shape
    f = w_gate.shape[-1]
    tm = _pick_tile(t, FFN_ROWS)
    tf = _pick_tile(f, 512)
    return pl.pallas_call(
        _ffn_kernel,
        grid=(t // tm, f // tf),
        in_specs=[
            pl.BlockSpec((tm, d), lambda i, j: (i, 0)),
            pl.BlockSpec((None, 1, d), lambda i, j: (layer, 0, 0)),
            pl.BlockSpec((None, d, tf), lambda i, j: (layer, 0, j)),
            pl.BlockSpec((None, d, tf), lambda i, j: (layer, 0, j)),
            pl.BlockSpec((None, tf, d), lambda i, j: (layer, j, 0)),
        ],
        out_specs=pl.BlockSpec((tm, d), lambda i, j: (i, 0)),
        out_shape=jax.ShapeDtypeStruct((t, d), F32),
        scratch_shapes=[pltpu.VMEM((tm, d), BF16)],
        compiler_params=_params("parallel", "arbitrary"),
        name="ffn",
    )(x, gain.reshape(gain.shape[0], 1, d), w_gate, w_up, w_down)


def _norm_matmul_kernel(x_ref, g_ref, w_ref, o_ref, h_scr):
    @pl.when(pl.program_id(1) == 0)
    def _():
        h_scr[...] = _rms_rows(x_ref[...].astype(F32), g_ref[...]).astype(BF16)

    o_ref[...] = _dot(h_scr[...], w_ref[...]).astype(o_ref.dtype)


def _norm_matmul(x, gain, w, *, col_block=0, out_dtype=BF16, name="norm_matmul"):
    m = x.shape[0]
    k, n = w.shape
    tm = _pick_tile(m, 1024)
    tn = n if n <= 2048 else _pick_tile(n, 1024 if k > 1024 else 2048)
    return pl.pallas_call(
        _norm_matmul_kernel,
        grid=(m // tm, n // tn),
        in_specs=[
            pl.BlockSpec((tm, k), lambda i, j: (i, col_block)),
            pl.BlockSpec((1, k), lambda i, j: (0, 0)),
            pl.BlockSpec((k, tn), lambda i, j: (0, j)),
        ],
        out_specs=pl.BlockSpec((tm, tn), lambda i, j: (i, j)),
        out_shape=jax.ShapeDtypeStruct((m, n), out_dtype),
        scratch_shapes=[pltpu.VMEM((tm, k), BF16)],
        compiler_params=_params("parallel", "arbitrary"),
        name=name,
    )(x, gain.reshape(1, k), w)


def _out_proj_kernel(a_ref, b_ref, wa_ref, wb_ref, x_ref, o_ref):
    acc = _dot(a_ref[...], wa_ref[...])
    acc = acc + _dot(b_ref[...], wb_ref[...])
    o_ref[...] = x_ref[...] + acc


def _out_proj(mix, mem_out, w_mix, w_mem, x):
    t, d = x.shape
    ka = mix.shape[1]
    kb = mem_out.shape[1]
    tm = _pick_tile(t, 512)
    tn = _pick_tile(d, 2048)
    return pl.pallas_call(
        _out_proj_kernel,
        grid=(t // tm, d // tn),
        in_specs=[
            pl.BlockSpec((tm, ka), lambda i, j: (i, 0)),
            pl.BlockSpec((tm, kb), lambda i, j: (i, 0)),
            pl.BlockSpec((ka, tn), lambda i, j: (0, j)),
            pl.BlockSpec((kb, tn), lambda i, j: (0, j)),
            pl.BlockSpec((tm, tn), lambda i, j: (i, j)),
        ],
        out_specs=pl.BlockSpec((tm, tn), lambda i, j: (i, j)),
        out_shape=jax.ShapeDtypeStruct((t, d), F32),
        compiler_params=_params("parallel", "parallel"),
        name="out_proj",
    )(mix, mem_out, w_mix, w_mem, x)


def _mem_attn_kernel(q_ref, kv_ref, gq_ref, gk_ref, o_ref):
    scale = 1.0 / math.sqrt(MEM_HEAD_DIM)
    for h in range(MEM_HEADS):
        cols = slice(h * MEM_HEAD_DIM, (h + 1) * MEM_HEAD_DIM)
        vcols = slice(MEM_WIDTH + h * MEM_HEAD_DIM, MEM_WIDTH + (h + 1) * MEM_HEAD_DIM)
        q = (_rms_rows(q_ref[:, cols].astype(F32), gq_ref[...]) * scale).astype(BF16)
        k = _rms_rows(kv_ref[:, cols].astype(F32), gk_ref[...]).astype(BF16)
        s = _dot_nt(q, k)
        m = jnp.max(s, axis=-1, keepdims=True)
        p = jnp.exp(s - m)
        l = jnp.sum(p, axis=-1, keepdims=True)
        o = _dot(p.astype(BF16), kv_ref[:, vcols]) / l
        o_ref[:, cols] = o.astype(o_ref.dtype)


def _mem_attn(u, q_col_block, mkv, gq, gk, batch):
    t = u.shape[0]
    s = t // batch
    m_len = mkv.shape[0] // batch
    tq = _pick_tile(s, 512)
    nq = s // tq
    return pl.pallas_call(
        _mem_attn_kernel,
        grid=(batch, nq),
        in_specs=[
            pl.BlockSpec((tq, MEM_WIDTH), lambda b, i: (b * nq + i, q_col_block)),
            pl.BlockSpec((m_len, 2 * MEM_WIDTH), lambda b, i: (b, 0)),
            pl.BlockSpec((1, MEM_HEAD_DIM), lambda b, i: (0, 0)),
            pl.BlockSpec((1, MEM_HEAD_DIM), lambda b, i: (0, 0)),
        ],
        out_specs=pl.BlockSpec((tq, MEM_WIDTH), lambda b, i: (b * nq + i, 0)),
        out_shape=jax.ShapeDtypeStruct((t, MEM_WIDTH), BF16),
        compiler_params=_params("parallel", "parallel"),
        name="mem_attn",
    )(u, mkv, gq.reshape(1, MEM_HEAD_DIM), gk.reshape(1, MEM_HEAD_DIM))


CONV_HALO = 16


def _conv_kernel(gb_ref, gc_ref, xt_ref, gch_ref, xth_ref, w_ref, o_ref, *, tiles_per_seq):
    i = pl.program_id(0)
    z = gc_ref[...].astype(F32) * xt_ref[...].astype(F32)
    zh = gch_ref[...].astype(F32) * xth_ref[...].astype(F32)
    zh = jnp.where(i % tiles_per_seq == 0, 0.0, zh)
    zc = jnp.concatenate([zh[CONV_HALO - 8:], z], axis=0)
    z1 = pltpu.roll(zc, 1, 0)[8:]
    z2 = pltpu.roll(zc, 2, 0)[8:]
    w = w_ref[...]
    conv = z2 * w[0:1] + z1 * w[1:2] + z * w[2:3]
    o_ref[...] = (gb_ref[...].astype(F32) * conv).astype(o_ref.dtype)


def _conv_mixer(u, conv_w, batch):
    t = u.shape[0]
    s = t // batch
    c = conv_w.shape[1]
    ts = _pick_tile(s, 512)
    tc = _pick_tile(c, 512)
    nc = c // tc
    halo_per_tile = ts // CONV_HALO

    def halo_map(col0):
        return lambda i, j: (jnp.maximum(i * halo_per_tile - 1, 0), col0 + j)

    return pl.pallas_call(
        functools.partial(_conv_kernel, tiles_per_seq=s // ts),
        grid=(t // ts, nc),
        in_specs=[
            pl.BlockSpec((ts, tc), lambda i, j: (i, j)),
            pl.BlockSpec((ts, tc), lambda i, j: (i, nc + j)),
            pl.BlockSpec((ts, tc), lambda i, j: (i, 2 * nc + j)),
            pl.BlockSpec((CONV_HALO, tc), halo_map(nc)),
            pl.BlockSpec((CONV_HALO, tc), halo_map(2 * nc)),
            pl.BlockSpec((CONV_K, tc), lambda i, j: (0, j)),
        ],
        out_specs=pl.BlockSpec((ts, tc), lambda i, j: (i, j)),
        out_shape=jax.ShapeDtypeStruct((t, c), BF16),
        compiler_params=_params("parallel", "parallel"),
        name="conv_mixer",
    )(u, u, u, u, u, conv_w)


def _online_softmax_step(s, v, m_ref, l_ref, acc_ref):
    m_prev = m_ref[...]
    m_new = jnp.maximum(m_prev, jnp.max(s, axis=-1, keepdims=True))
    alpha = jnp.exp2(m_prev - m_new)
    p = [jnp.exp2(s[:, t * LANES:(t + 1) * LANES] - m_new) for t in range(s.shape[1] // LANES)]
    l_ref[...] = alpha * l_ref[...] + functools.reduce(lambda a, b: a + b, p)
    acc_ref[...] = alpha * acc_ref[...] + _dot(jnp.concatenate(p, axis=-1).astype(BF16), v)
    m_ref[...] = m_new


def _softmax_finish(l_ref, acc_ref):
    return acc_ref[...] / jnp.sum(l_ref[...], axis=-1, keepdims=True)


def _causal_mask(s):
    row = lax.broadcasted_iota(jnp.int32, s.shape, 0)
    col = lax.broadcasted_iota(jnp.int32, s.shape, 1)
    return jnp.where(col <= row, s, NEG_BIG)


def _swap_rope_halves(t):
    lane = lax.broadcasted_iota(jnp.int32, t.shape, 1)
    up = pltpu.roll(t, MLA_ROPE // 2, 1)
    down = pltpu.roll(t, LANES - MLA_ROPE // 2, 1)
    return jnp.where(lane < MLA_ROPE // 2, down, up)


def _rope_lanes(t, cos_t, sin_t):
    return t * cos_t + _swap_rope_halves(t) * sin_t


def _rms_rope_part(x, gain):
    r = lax.rsqrt(jnp.sum(x * x, axis=-1, keepdims=True) * (1.0 / MLA_ROPE) + RMS_EPS)
    return x * r * gain


MLA_HEADS_PER_STEP = 4


def _mla_attn_kernel(q_ref, kn_ref, v_ref, kr_ref, cosq_ref, sinq_ref, cosk_ref, sink_ref,
                     gq_ref, gk_ref, o_ref, k_scr, *state_scr, tq):
    i = pl.program_id(2)
    state = [state_scr[3 * e:3 * e + 3] for e in range(MLA_HEADS_PER_STEP)]

    @pl.when(i == 0)
    def _():
        kr = _rms_rope_part(kr_ref[...].astype(F32), gk_ref[:, LANES:])
        kr = _rope_lanes(kr, cosk_ref[...], sink_ref[...]).astype(BF16)
        for e in range(MLA_HEADS_PER_STEP):
            kn = kn_ref[:, e * LANES:(e + 1) * LANES].astype(F32)
            k_scr[e, :, :LANES] = _rms_rows(kn, gk_ref[:, :LANES]).astype(BF16)
            k_scr[e, :, LANES:] = kr

    scale = LOG2_E / math.sqrt(MLA_QK)
    q_heads = []
    for e in range(MLA_HEADS_PER_STEP):
        c0 = e * MLA_QPAD
        q_nope = _rms_rows(q_ref[:, c0:c0 + LANES].astype(F32), gq_ref[:, :LANES])
        q_rot = _rms_rope_part(q_ref[:, c0 + LANES:c0 + MLA_QPAD].astype(F32), gq_ref[:, LANES:])
        q_rot = _rope_lanes(q_rot, cosq_ref[...], sinq_ref[...])
        q_heads.append((jnp.concatenate([q_nope, q_rot], axis=-1) * scale).astype(BF16))

    for m_scr, l_scr, acc_scr in state:
        m_scr[...] = jnp.full(m_scr.shape, NEG_BIG, F32)
        l_scr[...] = jnp.zeros(l_scr.shape, F32)
        acc_scr[...] = jnp.zeros(acc_scr.shape, F32)

    def chunk(j, masked):
        rows = pl.ds(pl.multiple_of(j * tq, tq), tq)
        for e in range(MLA_HEADS_PER_STEP):
            s = _dot_nt(q_heads[e], k_scr[e, rows, :])
            if masked:
                s = _causal_mask(s)
            _online_softmax_step(s, v_ref[rows, e * MLA_V:(e + 1) * MLA_V], *state[e])

    def body(j, carry):
        chunk(j, False)
        return carry

    lax.fori_loop(0, i, body, 0)
    chunk(i, True)
    for e, (_, l_scr, acc_scr) in enumerate(state):
        o_ref[:, e * MLA_V:(e + 1) * MLA_V] = _softmax_finish(l_scr, acc_scr).astype(o_ref.dtype)


def _mla_attention(q, kv, u, krope_block, cos_t, sin_t, gq, gk, batch):
    t = q.shape[0]
    s = t // batch
    tq = 512 if s % 512 == 0 and s > 512 else BLOCK
    nq = s // tq
    h = MLA_HEADS
    hs = MLA_HEADS_PER_STEP
    groups = h // hs
    kernel = functools.partial(_mla_attn_kernel, tq=tq)
    return pl.pallas_call(
        kernel,
        grid=(batch, groups, nq),
        in_specs=[
            pl.BlockSpec((tq, hs * MLA_QPAD), lambda b, hh, i: (b * nq + i, hh)),
            pl.BlockSpec((s, hs * LANES), lambda b, hh, i: (b, hh)),
            pl.BlockSpec((s, hs * LANES), lambda b, hh, i: (b, groups + hh)),
            pl.BlockSpec((s, LANES), lambda b, hh, i: (b, krope_block)),
            pl.BlockSpec((tq, LANES), lambda b, hh, i: (i, 0)),
            pl.BlockSpec((tq, LANES), lambda b, hh, i: (i, 0)),
            pl.BlockSpec((s, LANES), lambda b, hh, i: (0, 0)),
            pl.BlockSpec((s, LANES), lambda b, hh, i: (0, 0)),
            pl.BlockSpec((1, MLA_QPAD), lambda b, hh, i: (0, 0)),
            pl.BlockSpec((1, MLA_QPAD), lambda b, hh, i: (0, 0)),
        ],
        out_specs=pl.BlockSpec((tq, hs * MLA_V), lambda b, hh, i: (b * nq + i, hh)),
        out_shape=jax.ShapeDtypeStruct((t, h * MLA_V), BF16),
        scratch_shapes=[pltpu.VMEM((hs, s, MLA_QPAD), BF16)] + hs * [
            pltpu.VMEM((tq, LANES), F32),
            pltpu.VMEM((tq, LANES), F32),
            pltpu.VMEM((tq, MLA_V), F32),
        ],
        compiler_params=_params("parallel", "parallel", "arbitrary"),
        name="mla_attention",
    )(q, kv, kv, u, cos_t, sin_t, cos_t, sin_t, gq, gk)


def _rope_tables(s):
    half = MLA_ROPE // 2
    inv = ROPE_THETA ** (-jnp.arange(half, dtype=F32) / half)
    ang = jnp.arange(s).astype(F32)[:, None] * inv
    cos, sin = jnp.cos(ang), jnp.sin(ang)
    zeros = jnp.zeros((s, LANES - MLA_ROPE), F32)
    return (jnp.concatenate([cos, cos, zeros], axis=-1), jnp.concatenate([-sin, sin, zeros], axis=-1))


SWA_CHUNKS = SWA_GROUP * SWA_HEAD_DIM // LANES


def _swa_kernel(sinks_ref, q_ref, kp_ref, kc_ref, vp_ref, vc_ref, bias_ref, gq_ref, gk_ref, o_ref):
    kh = pl.program_id(1)
    n = pl.program_id(2)
    kv_half = kh % 2
    kv_is_low = kv_half == 0
    lane = lax.broadcasted_iota(jnp.int32, (BLOCK, LANES), 1)
    own = lane // HALF_LANES == kv_half
    scale = 1.0 / math.sqrt(SWA_HEAD_DIM)

    q = jnp.concatenate(
        [_rms_half_lanes(q_ref[:, c * LANES:(c + 1) * LANES].astype(F32), gq_ref[...]) * scale
         for c in range(SWA_CHUNKS)], axis=0).astype(BF16)

    def by_parity(x):
        mine = jnp.where(own, x, 0.0)
        other = pltpu.roll(mine, HALF_LANES, 1)
        return (jnp.where(kv_is_low, mine, other).astype(BF16), jnp.where(kv_is_low, other, mine).astype(BF16))

    kp = by_parity(_rms_half_lanes(kp_ref[...].astype(F32), gk_ref[...]))
    kc = by_parity(_rms_half_lanes(kc_ref[...].astype(F32), gk_ref[...]))
    vp = by_parity(vp_ref[...].astype(F32))
    vc = by_parity(vc_ref[...].astype(F32))

    row_chunk = lax.broadcasted_iota(jnp.int32, (SWA_CHUNKS * BLOCK, 1), 0) // BLOCK
    out = None
    for e in range(2):
        sink = jnp.zeros((SWA_CHUNKS * BLOCK, 1), F32)
        for c in range(SWA_CHUNKS):
            sink = jnp.where(row_chunk == c, sinks_ref[kh * SWA_GROUP + 2 * c + e], sink)
        s_prev = jnp.where(n > 0, _dot_nt(q, kp[e]) + bias_ref[0, e, 0], NEG_BIG)
        s_cur = _dot_nt(q, kc[e]) + bias_ref[0, e, 1]
        m = jnp.maximum(jnp.max(jnp.maximum(s_prev, s_cur), axis=-1, keepdims=True), sink)
        p_prev = jnp.exp(s_prev - m)
        p_cur = jnp.exp(s_cur - m)
        l = jnp.sum(p_prev + p_cur, axis=-1, keepdims=True) + jnp.exp(sink - m)
        o = (_dot(p_prev.astype(BF16), vp[e]) + _dot(p_cur.astype(BF16), vc[e])) / l
        out = o if out is None else out + o
    for c in range(SWA_CHUNKS):
        o_ref[:, c * LANES:(c + 1) * LANES] = out[c * BLOCK:(c + 1) * BLOCK].astype(o_ref.dtype)


def _t5_causal_bucket(dist):
    exact = REL_BUCKETS // 2
    d = np.maximum(dist, 0)
    log_b = exact + (np.log(np.maximum(d, 1) / exact) / np.log(REL_MAX_DIST / exact)
                     * (REL_BUCKETS - exact)).astype(np.int32)
    log_b = np.minimum(log_b, REL_BUCKETS - 1)
    return np.where(d < exact, d, log_b).astype(np.int32)


def _swa_bias_table(rel_bias):
    hq = rel_bias.shape[1]
    by_dist = rel_bias.astype(F32)[_t5_causal_bucket(np.arange(WINDOW))]
    masked = lambda width: jnp.full((hq, width), NEG_BIG, F32)
    n = 3 * BLOCK - 1
    w = jnp.concatenate([masked(BLOCK), by_dist[::-1].T, masked(BLOCK - 1)], axis=1)
    u = jnp.roll(w, -(BLOCK - 1), axis=1)
    flat = jnp.broadcast_to(u[:, None, :], (hq, BLOCK, n)).reshape(hq, BLOCK * n)
    bias = flat[:, :BLOCK * (n - 1)].reshape(hq, BLOCK, n - 1)[:, :, :2 * BLOCK]
    bias = bias.reshape(SWA_KV_HEADS, SWA_CHUNKS, 2, BLOCK, 2, BLOCK)
    bias = bias.transpose(0, 2, 4, 1, 3, 5)
    return bias.reshape(SWA_KV_HEADS, 2, 2, SWA_CHUNKS * BLOCK, BLOCK)


def _swa_attention(u, q_norm, k_norm, sinks, rel_bias, batch):
    t = u.shape[0]
    s = t // batch
    nblk = s // BLOCK
    nq_cols = SWA_Q_HEADS * SWA_HEAD_DIM
    k_block0 = nq_cols // LANES
    v_block0 = (nq_cols + SWA_KV_HEADS * SWA_HEAD_DIM) // LANES
    group_cols = SWA_CHUNKS * LANES
    gq = jnp.tile(q_norm.reshape(1, SWA_HEAD_DIM), (1, 2))
    gk = jnp.tile(k_norm.reshape(1, SWA_HEAD_DIM), (1, 2))

    def prev_map(col0):
        return lambda b, kh, n: (b * nblk + jnp.maximum(n - 1, 0), col0 + kh // 2)

    def cur_map(col0):
        return lambda b, kh, n: (b * nblk + n, col0 + kh // 2)

    return pl.pallas_call(
        _swa_kernel,
        grid=(batch, SWA_KV_HEADS, nblk),
        in_specs=[
            pl.BlockSpec(memory_space=pltpu.SMEM),
            pl.BlockSpec((BLOCK, group_cols), lambda b, kh, n: (b * nblk + n, kh)),
            pl.BlockSpec((BLOCK, LANES), prev_map(k_block0)),
            pl.BlockSpec((BLOCK, LANES), cur_map(k_block0)),
            pl.BlockSpec((BLOCK, LANES), prev_map(v_block0)),
            pl.BlockSpec((BLOCK, LANES), cur_map(v_block0)),
            pl.BlockSpec((1, 2, 2, SWA_CHUNKS * BLOCK, BLOCK), lambda b, kh, n: (kh, 0, 0, 0, 0)),
            pl.BlockSpec((1, LANES), lambda b, kh, n: (0, 0)),
            pl.BlockSpec((1, LANES), lambda b, kh, n: (0, 0)),
        ],
        out_specs=pl.BlockSpec((BLOCK, group_cols), lambda b, kh, n: (b * nblk + n, kh)),
        out_shape=jax.ShapeDtypeStruct((t, nq_cols), BF16),
        compiler_params=_params("parallel", "parallel", "parallel"),
        name="swa_attention",
    )(sinks.astype(F32), u, u, u, u, u, _swa_bias_table(rel_bias), gq, gk)


def _fox_gate_kernel(u_ref, b_ref, o_ref):
    x = u_ref[...] + b_ref[...]
    log_f = -(jnp.maximum(-x, 0.0) + jnp.log1p(jnp.exp(-jnp.abs(x))))
    row = lax.broadcasted_iota(jnp.int32, log_f.shape, 0)
    c = log_f
    shift = 1
    while shift < c.shape[0]:
        c = c + jnp.where(row >= shift, pltpu.roll(c, shift, 0), 0.0)
        shift *= 2
    o_ref[...] = c


def _fox_gate_cumsum(u_f, b_f, batch):
    t = u_f.shape[0]
    s = t // batch
    return pl.pallas_call(
        _fox_gate_kernel,
        grid=(batch,),
        in_specs=[pl.BlockSpec((s, LANES), lambda b: (b, 0)), pl.BlockSpec((1, LANES), lambda b: (0, 0))],
        out_specs=pl.BlockSpec((s, LANES), lambda b: (b, 0)),
        out_shape=jax.ShapeDtypeStruct((t, LANES), F32),
        compiler_params=_params("parallel"),
        name="fox_gate_cumsum",
    )(u_f, b_f)


FOX_GATE_PIECES = 3


def _split_bf16_pieces(x):
    pieces = []
    for _ in range(FOX_GATE_PIECES - 1):
        piece = x.astype(BF16).astype(F32)
        pieces.append(piece)
        x = x - piece
    return pieces + [x]


def _fox_operand(x, gate_col, lane, head, is_query):
    base = (1 - head) * HALF_LANES
    out = jnp.where(lane // HALF_LANES == head, x, 0.0)
    for t, piece in enumerate(_split_bf16_pieces(gate_col)):
        gate_lane, one_lane = base + t, base + FOX_GATE_PIECES + t
        if not is_query:
            gate_lane, one_lane, piece = one_lane, gate_lane, -piece
        out = jnp.where(lane == gate_lane, piece, out)
        out = jnp.where(lane == one_lane, 1.0, out)
    return out.astype(BF16)


FOX_PAIRS_PER_STEP = 2


def _fox_attn_kernel(q_ref, k_ref, v_ref, cq_ref, ck_ref, gq_ref, gk_ref, o_ref, k_scr, *state_scr, tq):
    group = pl.program_id(1)
    i = pl.program_id(2)
    heads = [(p, e) for p in range(FOX_PAIRS_PER_STEP) for e in range(2)]
    state = [state_scr[3 * n:3 * n + 3] for n in range(len(heads))]

    def gate_col(c, p, e):
        lane = lax.broadcasted_iota(jnp.int32, c.shape, 1)
        head = 2 * (FOX_PAIRS_PER_STEP * group + p) + e
        return jnp.sum(jnp.where(lane == head, c, 0.0), axis=-1, keepdims=True) * LOG2_E

    def pair_cols(p):
        return slice(p * LANES, (p + 1) * LANES)

    @pl.when(i == 0)
    def _():
        for p in range(FOX_PAIRS_PER_STEP):
            kn = _rms_half_lanes(k_ref[:, pair_cols(p)].astype(F32), gk_ref[...])
            lane = lax.broadcasted_iota(jnp.int32, kn.shape, 1)
            for e in range(2):
                k_scr[2 * p + e] = _fox_operand(kn, gate_col(ck_ref[...], p, e), lane, e, False)

    scale = LOG2_E / math.sqrt(FOX_HEAD_DIM)
    lane = lax.broadcasted_iota(jnp.int32, (tq, LANES), 1)
    q_heads = []
    for p in range(FOX_PAIRS_PER_STEP):
        qn = _rms_half_lanes(q_ref[:, pair_cols(p)].astype(F32), gq_ref[...]) * scale
        q_heads += [_fox_operand(qn, gate_col(cq_ref[...], p, e), lane, e, True) for e in range(2)]

    for m_scr, l_scr, acc_scr in state:
        m_scr[...] = jnp.full(m_scr.shape, NEG_BIG, F32)
        l_scr[...] = jnp.zeros(l_scr.shape, F32)
        acc_scr[...] = jnp.zeros(acc_scr.shape, F32)

    def chunk(j, masked):
        rows = pl.ds(pl.multiple_of(j * tq, tq), tq)
        for n, (p, _) in enumerate(heads):
            s = _dot_nt(q_heads[n], k_scr[n, rows, :])
            if masked:
                s = _causal_mask(s)
            _online_softmax_step(s, v_ref[rows, pair_cols(p)], *state[n])

    def body(j, carry):
        chunk(j, False)
        return carry

    lax.fori_loop(0, i, body, 0)
    chunk(i, True)
    for p in range(FOX_PAIRS_PER_STEP):
        low, high = (_softmax_finish(*state[2 * p + e][1:]) for e in range(2))
        o_ref[:, pair_cols(p)] = jnp.where(lane < HALF_LANES, low, high).astype(o_ref.dtype)


def _fox_attention(u, c, q_norm, k_norm, batch):
    t = u.shape[0]
    s = t // batch
    tq = 512 if s % 512 == 0 and s > 512 else BLOCK
    nq = s // tq
    width = FOX_PAIRS_PER_STEP * LANES
    groups = FOX_WIDTH // width
    gq = jnp.tile(q_norm.reshape(1, FOX_HEAD_DIM), (1, 2))
    gk = jnp.tile(k_norm.reshape(1, FOX_HEAD_DIM), (1, 2))
    kernel = functools.partial(_fox_attn_kernel, tq=tq)
    return pl.pallas_call(
        kernel,
        grid=(batch, groups, nq),
        in_specs=[
            pl.BlockSpec((tq, width), lambda b, g, i: (b * nq + i, g)),
            pl.BlockSpec((s, width), lambda b, g, i: (b, groups + g)),
            pl.BlockSpec((s, width), lambda b, g, i: (b, 2 * groups + g)),
            pl.BlockSpec((tq, LANES), lambda b, g, i: (b * nq + i, 0)),
            pl.BlockSpec((s, LANES), lambda b, g, i: (b, 0)),
            pl.BlockSpec((1, LANES), lambda b, g, i: (0, 0)),
            pl.BlockSpec((1, LANES), lambda b, g, i: (0, 0)),
        ],
        out_specs=pl.BlockSpec((tq, width), lambda b, g, i: (b * nq + i, g)),
        out_shape=jax.ShapeDtypeStruct((t, FOX_WIDTH), BF16),
        scratch_shapes=[pltpu.VMEM((2 * FOX_PAIRS_PER_STEP, s, LANES), BF16)] + 2 * FOX_PAIRS_PER_STEP * [
            pltpu.VMEM((tq, LANES), F32),
            pltpu.VMEM((tq, LANES), F32),
            pltpu.VMEM((tq, LANES), F32),
        ],
        compiler_params=_params("parallel", "parallel", "arbitrary"),
        name="fox_attention",
    )(u, u, u, c, c, gq, gk)


def kernel(x, mem, norm_ffn1, ffn1_w_gate, ffn1_w_up, ffn1_w_down, norm_mix, norm_ffn2, ffn2_w_gate, ffn2_w_up, ffn2_w_down, norm_mem, mem_w_kv, mem_q_norm, mem_k_norm, conv_w_in, conv_w, conv_w_out, mla_w_in, mla_q_a_norm, mla_w_q_b, mla_kv_a_norm, mla_w_kv_b, mla_q_norm, mla_k_norm, mla_w_out, swa_w_in, swa_q_norm, swa_k_norm, swa_sinks, swa_w_out, rel_bias, fox_w_in, fox_b_f, fox_q_norm, fox_k_norm, fox_w_out):
    b, s, d = x.shape
    m_len = mem.shape[1]
    depth = norm_ffn1.shape[0]
    xt = x.reshape(b * s, d)
    mem2 = mem.reshape(b * m_len, d)
    bf = lambda w: w.astype(BF16)
    ffn1_w = (bf(ffn1_w_gate), bf(ffn1_w_up), bf(ffn1_w_down))
    ffn2_w = (bf(ffn2_w_gate), bf(ffn2_w_up), bf(ffn2_w_down))

    for i in range(depth):
        kind, occ = i % 4, i // 4
        xt = _ffn(xt, norm_ffn1, *ffn1_w, i)

        if kind == 0:
            c = conv_w.shape[-1]
            u = _norm_matmul(xt, norm_mix[i], bf(conv_w_in[occ]), name="conv_in_proj")
            memq_block = 3 * c // MEM_WIDTH
            mix = _conv_mixer(u, conv_w[occ], b)
            w_out = conv_w_out[occ]
        elif kind == 1:
            w_in = mla_w_in[occ]
            lat = MLA_Q_RANK + MLA_KV_RANK
            w_in = jnp.concatenate([w_in[:, :lat], w_in[:, lat + MLA_ROPE:], w_in[:, lat:lat + MLA_ROPE],
                                    jnp.zeros((d, LANES - MLA_ROPE), F32)], axis=1)
            u = _norm_matmul(xt, norm_mix[i], bf(w_in), name="mla_in_proj")
            memq_block = lat // MEM_WIDTH
            krope_block = (lat + MEM_WIDTH) // LANES
            wq = mla_w_q_b[occ].reshape(MLA_Q_RANK, MLA_HEADS, MLA_QK)
            wq = jnp.concatenate([wq, jnp.zeros((MLA_Q_RANK, MLA_HEADS, MLA_QPAD - MLA_QK), F32)], axis=-1)
            wkv = mla_w_kv_b[occ].reshape(MLA_KV_RANK, MLA_HEADS, MLA_NOPE + MLA_V)
            wkv = jnp.concatenate([wkv[..., :MLA_NOPE].reshape(MLA_KV_RANK, -1),
                                   wkv[..., MLA_NOPE:].reshape(MLA_KV_RANK, -1)], axis=1)
            q = _norm_matmul(u, mla_q_a_norm[occ], bf(wq.reshape(MLA_Q_RANK, -1)), col_block=0, name="mla_q_proj")
            kv = _norm_matmul(u, mla_kv_a_norm[occ], bf(wkv), col_block=1, name="mla_kv_proj")
            pad = jnp.zeros((MLA_QPAD - MLA_QK,), F32)
            gq = jnp.concatenate([mla_q_norm[occ], pad]).reshape(1, MLA_QPAD)
            gk = jnp.concatenate([mla_k_norm[occ], pad]).reshape(1, MLA_QPAD)
            cos_t, sin_t = _rope_tables(s)
            mix = _mla_attention(q, kv, u, krope_block, cos_t, sin_t, gq, gk, b)
            w_out = mla_w_out[occ]
        elif kind == 2:
            u = _norm_matmul(xt, norm_mix[i], bf(swa_w_in[occ]), name="swa_in_proj")
            memq_block = (SWA_Q_HEADS + 2 * SWA_KV_HEADS) * SWA_HEAD_DIM // MEM_WIDTH
            mix = _swa_attention(u, swa_q_norm[occ], swa_k_norm[occ], swa_sinks[occ], rel_bias, b)
            w_out = swa_w_out[occ]
        else:
            w_in = fox_w_in[occ]
            qkv = 3 * FOX_WIDTH
            w_main = jnp.concatenate([w_in[:, :qkv], w_in[:, qkv + FOX_HEADS:]], axis=1)
            w_gate = jnp.concatenate([w_in[:, qkv:qkv + FOX_HEADS], jnp.zeros((d, LANES - FOX_HEADS), F32)], axis=1)
            u = _norm_matmul(xt, norm_mix[i], bf(w_main), name="fox_in_proj")
            u_f = _norm_matmul(xt, norm_mix[i], bf(w_gate), out_dtype=F32, name="fox_gate_proj")
            b_f = jnp.concatenate([fox_b_f[occ], jnp.zeros((LANES - FOX_HEADS,), F32)]).reshape(1, LANES)
            c_gate = _fox_gate_cumsum(u_f, b_f, b)
            memq_block = qkv // MEM_WIDTH
            mix = _fox_attention(u, c_gate, fox_q_norm[occ], fox_k_norm[occ], b)
            w_out = fox_w_out[occ]

        mkv = _norm_matmul(mem2, norm_mem[i], bf(mem_w_kv[i]), name="mem_kv_proj")
        mem_out = _mem_attn(u, memq_block, mkv, mem_q_norm[i], mem_k_norm[i], b)
        k_mix = mix.shape[1]
        xt = _out_proj(mix, mem_out, bf(w_out[:k_mix]), bf(w_out[k_mix:]), xt)

        xt = _ffn(xt, norm_ffn2, *ffn2_w, i)
    return xt.reshape(b, s, d)
```

```python
import functools
import math

import numpy as np
import jax
import jax.numpy as jnp
from jax import lax
from jax.experimental import pallas as pl
from jax.experimental.pallas import tpu as pltpu

F32 = jnp.float32
BF16 = jnp.bfloat16

RMS_EPS = 1e-6
NEG_BIG = -1e30
LOG2_E = math.log2(math.e)
LANES = 128
HALF_LANES = LANES // 2
VMEM_LIMIT_BYTES = 56 * 1024 * 1024

BLOCK = 128
MEM_HEADS = 4
MEM_HEAD_DIM = 128
MEM_WIDTH = MEM_HEADS * MEM_HEAD_DIM
CONV_K = 3
MLA_HEADS = 16
MLA_Q_RANK = 512
MLA_KV_RANK = 512
MLA_NOPE = 128
MLA_ROPE = 64
MLA_V = 128
MLA_QK = MLA_NOPE + MLA_ROPE
MLA_QPAD = 2 * LANES
ROPE_THETA = 10000.0
SWA_Q_HEADS = 32
SWA_KV_HEADS = 4
SWA_GROUP = SWA_Q_HEADS // SWA_KV_HEADS
SWA_HEAD_DIM = 64
WINDOW = 128
REL_BUCKETS = 32
REL_MAX_DIST = 128
FOX_HEADS = 32
FOX_HEAD_DIM = 64
FOX_WIDTH = FOX_HEADS * FOX_HEAD_DIM


def _params(*semantics):
    return pltpu.CompilerParams(dimension_semantics=semantics, vmem_limit_bytes=VMEM_LIMIT_BYTES)


def _pick_tile(n, target):
    best = None
    for t in range(LANES, min(n, target) + 1, LANES):
        if n % t == 0:
            best = t
    return n if best is None else best


def _rms_rows(x, gain):
    r = lax.rsqrt(jnp.mean(x * x, axis=-1, keepdims=True) + RMS_EPS)
    return x * r * gain


def _rms_half_lanes(x, gain):
    lane = lax.broadcasted_iota(jnp.int32, x.shape, 1)
    low = lane < HALF_LANES
    sq = x * x
    s_low = jnp.sum(jnp.where(low, sq, 0.0), axis=-1, keepdims=True)
    s_high = jnp.sum(jnp.where(low, 0.0, sq), axis=-1, keepdims=True)
    r = lax.rsqrt(jnp.where(low, s_low, s_high) * (1.0 / HALF_LANES) + RMS_EPS)
    return x * r * gain


def _dot(a, b):
    return jnp.dot(a, b, preferred_element_type=F32)


def _dot_nt(a, b):
    return lax.dot_general(a, b, (((1,), (1,)), ((), ())), preferred_element_type=F32)


FFN_ROWS = 1024
FFN_SPLIT = 2


def _ffn_kernel(x_ref, g_ref, wg_ref, wu_ref, wd_ref, o_ref, h_scr):
    @pl.when(pl.program_id(1) == 0)
    def _():
        x = x_ref[...]
        h_scr[...] = _rms_rows(x, g_ref[...]).astype(BF16)
        o_ref[...] = x

    h = h_scr[...]
    width = wg_ref.shape[-1] // FFN_SPLIT
    acts = []
    for c in range(FFN_SPLIT):
        cols = slice(c * width, (c + 1) * width)
        gate = _dot(h, wg_ref[:, cols])
        up = _dot(h, wu_ref[:, cols])
        acts.append((gate / (1.0 + jnp.exp(-gate)) * (0.5 * up)).astype(BF16))
    o_ref[...] += _dot(jnp.concatenate(acts, axis=-1), wd_ref[...])


def _ffn(x, gain, w_gate, w_up, w_down, layer):
    t, d = x.shape
    f = w_gate.shape[-1]
    tm = _pick_tile(t, FFN_ROWS)
    tf = _pick_tile(f, 512)
    return pl.pallas_call(
        _ffn_kernel,
        grid=(t // tm, f // tf),
        in_specs=[
            pl.BlockSpec((tm, d), lambda i, j: (i, 0)),
            pl.BlockSpec((None, 1, d), lambda i, j: (layer, 0, 0)),
            pl.BlockSpec((None, d, tf), lambda i, j: (layer, 0, j)),
            pl.BlockSpec((None, d, tf), lambda i, j: (layer, 0, j)),
            pl.BlockSpec((None, tf, d), lambda i, j: (layer, j, 0)),
        ],
        out_specs=pl.BlockSpec((tm, d), lambda i, j: (i, 0)),
        out_shape=jax.ShapeDtypeStruct((t, d), F32),
        scratch_shapes=[pltpu.VMEM((tm, d), BF16)],
        compiler_params=_params("parallel", "arbitrary"),
        name="ffn",
    )(x, gain.reshape(gain.shape[0], 1, d), w_gate, w_up, w_down)


NORM_MATMUL_COLS_LONG_K = 1792
NORM_MATMUL_COLS_SHORT_K = 2048


def _norm_matmul_kernel(x_ref, g_ref, w_ref, o_ref, h_scr):
    @pl.when(pl.program_id(1) == 0)
    def _():
        h_scr[...] = _rms_rows(x_ref[...].astype(F32), g_ref[...]).astype(BF16)

    o_ref[...] = _dot(h_scr[...], w_ref[...]).astype(o_ref.dtype)


def _norm_matmul(x, gain, w, *, col_block=0, out_dtype=BF16, name="norm_matmul"):
    m = x.shape[0]
    k, n = w.shape
    tm = _pick_tile(m, 1024)
    tn = n if n <= 2048 else _pick_tile(n, NORM_MATMUL_COLS_LONG_K if k > 1024 else NORM_MATMUL_COLS_SHORT_K)
    return pl.pallas_call(
        _norm_matmul_kernel,
        grid=(m // tm, n // tn),
        in_specs=[
            pl.BlockSpec((tm, k), lambda i, j: (i, col_block)),
            pl.BlockSpec((1, k), lambda i, j: (0, 0)),
            pl.BlockSpec((k, tn), lambda i, j: (0, j)),
        ],
        out_specs=pl.BlockSpec((tm, tn), lambda i, j: (i, j)),
        out_shape=jax.ShapeDtypeStruct((m, n), out_dtype),
        scratch_shapes=[pltpu.VMEM((tm, k), BF16)],
        compiler_params=_params("parallel", "arbitrary"),
        name=name,
    )(x, gain.reshape(1, k), w)


def _out_proj_kernel(a_ref, b_ref, wa_ref, wb_ref, x_ref, o_ref):
    acc = _dot(a_ref[...], wa_ref[...])
    acc = acc + _dot(b_ref[...], wb_ref[...])
    o_ref[...] = x_ref[...] + acc


def _out_proj(mix, mem_out, w_mix, w_mem, x):
    t, d = x.shape
    ka = mix.shape[1]
    kb = mem_out.shape[1]
    tm = _pick_tile(t, 512)
    tn = _pick_tile(d, 2048)
    return pl.pallas_call(
        _out_proj_kernel,
        grid=(t // tm, d // tn),
        in_specs=[
            pl.BlockSpec((tm, ka), lambda i, j: (i, 0)),
            pl.BlockSpec((tm, kb), lambda i, j: (i, 0)),
            pl.BlockSpec((ka, tn), lambda i, j: (0, j)),
            pl.BlockSpec((kb, tn), lambda i, j: (0, j)),
            pl.BlockSpec((tm, tn), lambda i, j: (i, j)),
        ],
        out_specs=pl.BlockSpec((tm, tn), lambda i, j: (i, j)),
        out_shape=jax.ShapeDtypeStruct((t, d), F32),
        compiler_params=_params("parallel", "parallel"),
        name="out_proj",
    )(mix, mem_out, w_mix, w_mem, x)


def _mem_attn_kernel(q_ref, kv_ref, gq_ref, gk_ref, o_ref):
    scale = 1.0 / math.sqrt(MEM_HEAD_DIM)
    for h in range(MEM_HEADS):
        cols = slice(h * MEM_HEAD_DIM, (h + 1) * MEM_HEAD_DIM)
        vcols = slice(MEM_WIDTH + h * MEM_HEAD_DIM, MEM_WIDTH + (h + 1) * MEM_HEAD_DIM)
        q = (_rms_rows(q_ref[:, cols].astype(F32), gq_ref[...]) * scale).astype(BF16)
        k = _rms_rows(kv_ref[:, cols].astype(F32), gk_ref[...]).astype(BF16)
        s = _dot_nt(q, k)
        m = jnp.max(s, axis=-1, keepdims=True)
        p = jnp.exp(s - m)
        l = jnp.sum(p, axis=-1, keepdims=True)
        o = _dot(p.astype(BF16), kv_ref[:, vcols]) / l
        o_ref[:, cols] = o.astype(o_ref.dtype)


def _mem_attn(u, q_col_block, mkv, gq, gk, batch):
    t = u.shape[0]
    s = t // batch
    m_len = mkv.shape[0] // batch
    tq = _pick_tile(s, 512)
    nq = s // tq
    return pl.pallas_call(
        _mem_attn_kernel,
        grid=(batch, nq),
        in_specs=[
            pl.BlockSpec((tq, MEM_WIDTH), lambda b, i: (b * nq + i, q_col_block)),
            pl.BlockSpec((m_len, 2 * MEM_WIDTH), lambda b, i: (b, 0)),
            pl.BlockSpec((1, MEM_HEAD_DIM), lambda b, i: (0, 0)),
            pl.BlockSpec((1, MEM_HEAD_DIM), lambda b, i: (0, 0)),
        ],
        out_specs=pl.BlockSpec((tq, MEM_WIDTH), lambda b, i: (b * nq + i, 0)),
        out_shape=jax.ShapeDtypeStruct((t, MEM_WIDTH), BF16),
        compiler_params=_params("parallel", "parallel"),
        name="mem_attn",
    )(u, mkv, gq.reshape(1, MEM_HEAD_DIM), gk.reshape(1, MEM_HEAD_DIM))


CONV_HALO = 16


def _conv_kernel(gb_ref, gc_ref, xt_ref, gch_ref, xth_ref, w_ref, o_ref, *, tiles_per_seq):
    i = pl.program_id(0)
    z = gc_ref[...].astype(F32) * xt_ref[...].astype(F32)
    zh = gch_ref[...].astype(F32) * xth_ref[...].astype(F32)
    zh = jnp.where(i % tiles_per_seq == 0, 0.0, zh)
    zc = jnp.concatenate([zh[CONV_HALO - 8:], z], axis=0)
    z1 = pltpu.roll(zc, 1, 0)[8:]
    z2 = pltpu.roll(zc, 2, 0)[8:]
    w = w_ref[...]
    conv = z2 * w[0:1] + z1 * w[1:2] + z * w[2:3]
    o_ref[...] = (gb_ref[...].astype(F32) * conv).astype(o_ref.dtype)


def _conv_mixer(u, conv_w, batch):
    t = u.shape[0]
    s = t // batch
    c = conv_w.shape[1]
    ts = _pick_tile(s, 512)
    tc = _pick_tile(c, 512)
    nc = c // tc
    halo_per_tile = ts // CONV_HALO

    def halo_map(col0):
        return lambda i, j: (jnp.maximum(i * halo_per_tile - 1, 0), col0 + j)

    return pl.pallas_call(
        functools.partial(_conv_kernel, tiles_per_seq=s // ts),
        grid=(t // ts, nc),
        in_specs=[
            pl.BlockSpec((ts, tc), lambda i, j: (i, j)),
            pl.BlockSpec((ts, tc), lambda i, j: (i, nc + j)),
            pl.BlockSpec((ts, tc), lambda i, j: (i, 2 * nc + j)),
            pl.BlockSpec((CONV_HALO, tc), halo_map(nc)),
            pl.BlockSpec((CONV_HALO, tc), halo_map(2 * nc)),
            pl.BlockSpec((CONV_K, tc), lambda i, j: (0, j)),
        ],
        out_specs=pl.BlockSpec((ts, tc), lambda i, j: (i, j)),
        out_shape=jax.ShapeDtypeStruct((t, c), BF16),
        compiler_params=_params("parallel", "parallel"),
        name="conv_mixer",
    )(u, u, u, u, u, conv_w)


def _online_softmax_step(s, v, m_ref, l_ref, acc_ref):
    m_prev = m_ref[...]
    m_new = jnp.maximum(m_prev, jnp.max(s, axis=-1, keepdims=True))
    alpha = jnp.exp2(m_prev - m_new)
    p = [jnp.exp2(s[:, t * LANES:(t + 1) * LANES] - m_new) for t in range(s.shape[1] // LANES)]
    l_ref[...] = alpha * l_ref[...] + functools.reduce(lambda a, b: a + b, p)
    acc_ref[...] = alpha * acc_ref[...] + _dot(jnp.concatenate(p, axis=-1).astype(BF16), v)
    m_ref[...] = m_new


def _softmax_finish(l_ref, acc_ref):
    return acc_ref[...] / jnp.sum(l_ref[...], axis=-1, keepdims=True)


def _causal_mask(s):
    row = lax.broadcasted_iota(jnp.int32, s.shape, 0)
    col = lax.broadcasted_iota(jnp.int32, s.shape, 1)
    return jnp.where(col <= row, s, NEG_BIG)


def _swap_rope_halves(t):
    lane = lax.broadcasted_iota(jnp.int32, t.shape, 1)
    up = pltpu.roll(t, MLA_ROPE // 2, 1)
    down = pltpu.roll(t, LANES - MLA_ROPE // 2, 1)
    return jnp.where(lane < MLA_ROPE // 2, down, up)


def _rope_lanes(t, cos_t, sin_t):
    return t * cos_t + _swap_rope_halves(t) * sin_t


def _rms_rope_part(x, gain):
    r = lax.rsqrt(jnp.sum(x * x, axis=-1, keepdims=True) * (1.0 / MLA_ROPE) + RMS_EPS)
    return x * r * gain


MLA_HEADS_PER_STEP = 8


def _mla_attn_kernel(q_ref, kn_ref, v_ref, kr_ref, cosq_ref, sinq_ref, cosk_ref, sink_ref,
                     gq_ref, gk_ref, o_ref, k_scr, *state_scr, tq):
    i = pl.program_id(2)
    state = [state_scr[3 * e:3 * e + 3] for e in range(MLA_HEADS_PER_STEP)]

    @pl.when(i == 0)
    def _():
        kr = _rms_rope_part(kr_ref[...].astype(F32), gk_ref[:, LANES:])
        kr = _rope_lanes(kr, cosk_ref[...], sink_ref[...]).astype(BF16)
        for e in range(MLA_HEADS_PER_STEP):
            kn = kn_ref[:, e * LANES:(e + 1) * LANES].astype(F32)
            k_scr[e, :, :LANES] = _rms_rows(kn, gk_ref[:, :LANES]).astype(BF16)
            k_scr[e, :, LANES:] = kr

    scale = LOG2_E / math.sqrt(MLA_QK)
    q_heads = []
    for e in range(MLA_HEADS_PER_STEP):
        c0 = e * MLA_QPAD
        q_nope = _rms_rows(q_ref[:, c0:c0 + LANES].astype(F32), gq_ref[:, :LANES])
        q_rot = _rms_rope_part(q_ref[:, c0 + LANES:c0 + MLA_QPAD].astype(F32), gq_ref[:, LANES:])
        q_rot = _rope_lanes(q_rot, cosq_ref[...], sinq_ref[...])
        q_heads.append((jnp.concatenate([q_nope, q_rot], axis=-1) * scale).astype(BF16))

    for m_scr, l_scr, acc_scr in state:
        m_scr[...] = jnp.full(m_scr.shape, NEG_BIG, F32)
        l_scr[...] = jnp.zeros(l_scr.shape, F32)
        acc_scr[...] = jnp.zeros(acc_scr.shape, F32)

    def chunk(j, masked):
        rows = pl.ds(pl.multiple_of(j * tq, tq), tq)
        for e in range(MLA_HEADS_PER_STEP):
            s = _dot_nt(q_heads[e], k_scr[e, rows, :])
            if masked:
                s = _causal_mask(s)
            _online_softmax_step(s, v_ref[rows, e * MLA_V:(e + 1) * MLA_V], *state[e])

    def body(j, carry):
        chunk(j, False)
        return carry

    lax.fori_loop(0, i, body, 0)
    chunk(i, True)
    for e, (_, l_scr, acc_scr) in enumerate(state):
        o_ref[:, e * MLA_V:(e + 1) * MLA_V] = _softmax_finish(l_scr, acc_scr).astype(o_ref.dtype)


def _mla_attention(q, kv, u, krope_block, cos_t, sin_t, gq, gk, batch):
    t = q.shape[0]
    s = t // batch
    tq = 512 if s % 512 == 0 and s > 512 else BLOCK
    nq = s // tq
    h = MLA_HEADS
    hs = MLA_HEADS_PER_STEP
    groups = h // hs
    kernel = functools.partial(_mla_attn_kernel, tq=tq)
    return pl.pallas_call(
        kernel,
        grid=(batch, groups, nq),
        in_specs=[
            pl.BlockSpec((tq, hs * MLA_QPAD), lambda b, hh, i: (b * nq + i, hh)),
            pl.BlockSpec((s, hs * LANES), lambda b, hh, i: (b, hh)),
            pl.BlockSpec((s, hs * LANES), lambda b, hh, i: (b, groups + hh)),
            pl.BlockSpec((s, LANES), lambda b, hh, i: (b, krope_block)),
            pl.BlockSpec((tq, LANES), lambda b, hh, i: (i, 0)),
            pl.BlockSpec((tq, LANES), lambda b, hh, i: (i, 0)),
            pl.BlockSpec((s, LANES), lambda b, hh, i: (0, 0)),
            pl.BlockSpec((s, LANES), lambda b, hh, i: (0, 0)),
            pl.BlockSpec((1, MLA_QPAD), lambda b, hh, i: (0, 0)),
            pl.BlockSpec((1, MLA_QPAD), lambda b, hh, i: (0, 0)),
        ],
        out_specs=pl.BlockSpec((tq, hs * MLA_V), lambda b, hh, i: (b * nq + i, hh)),
        out_shape=jax.ShapeDtypeStruct((t, h * MLA_V), BF16),
        scratch_shapes=[pltpu.VMEM((hs, s, MLA_QPAD), BF16)] + hs * [
            pltpu.VMEM((tq, LANES), F32),
            pltpu.VMEM((tq, LANES), F32),
            pltpu.VMEM((tq, MLA_V), F32),
        ],
        compiler_params=_params("parallel", "parallel", "arbitrary"),
        name="mla_attention",
    )(q, kv, kv, u, cos_t, sin_t, cos_t, sin_t, gq, gk)


def _rope_tables(s):
    half = MLA_ROPE // 2
    inv = ROPE_THETA ** (-jnp.arange(half, dtype=F32) / half)
    ang = jnp.arange(s).astype(F32)[:, None] * inv
    cos, sin = jnp.cos(ang), jnp.sin(ang)
    zeros = jnp.zeros((s, LANES - MLA_ROPE), F32)
    return (jnp.concatenate([cos, cos, zeros], axis=-1), jnp.concatenate([-sin, sin, zeros], axis=-1))


SWA_CHUNKS = SWA_GROUP * SWA_HEAD_DIM // LANES


def _swa_kernel(sinks_ref, q_ref, kp_ref, kc_ref, vp_ref, vc_ref, bias_ref, gq_ref, gk_ref, o_ref):
    pair = pl.program_id(1)
    n = pl.program_id(2)
    lane = lax.broadcasted_iota(jnp.int32, (BLOCK, LANES), 1)
    scale = 1.0 / math.sqrt(SWA_HEAD_DIM)
    group_cols = SWA_CHUNKS * LANES
    row_chunk = lax.broadcasted_iota(jnp.int32, (SWA_CHUNKS * BLOCK, 1), 0) // BLOCK

    kp_n = _rms_half_lanes(kp_ref[...].astype(F32), gk_ref[...])
    kc_n = _rms_half_lanes(kc_ref[...].astype(F32), gk_ref[...])
    vp_f = vp_ref[...].astype(F32)
    vc_f = vc_ref[...].astype(F32)

    for half in range(2):
        kh = 2 * pair + half
        own = lane // HALF_LANES == half
        col0 = half * group_cols

        def by_parity(x):
            mine = jnp.where(own, x, 0.0)
            other = pltpu.roll(mine, HALF_LANES, 1)
            low, high = (mine, other) if half == 0 else (other, mine)
            return low.astype(BF16), high.astype(BF16)

        kp, kc, vp, vc = by_parity(kp_n), by_parity(kc_n), by_parity(vp_f), by_parity(vc_f)
        q = jnp.concatenate(
            [_rms_half_lanes(q_ref[:, col0 + c * LANES:col0 + (c + 1) * LANES].astype(F32), gq_ref[...]) * scale
             for c in range(SWA_CHUNKS)], axis=0).astype(BF16)

        out = None
        for e in range(2):
            sink = jnp.zeros((SWA_CHUNKS * BLOCK, 1), F32)
            for c in range(SWA_CHUNKS):
                sink = jnp.where(row_chunk == c, sinks_ref[kh * SWA_GROUP + 2 * c + e], sink)
            s_prev = jnp.where(n > 0, _dot_nt(q, kp[e]) + bias_ref[half, e, 0], NEG_BIG)
            s_cur = _dot_nt(q, kc[e]) + bias_ref[half, e, 1]
            m = jnp.maximum(jnp.max(jnp.maximum(s_prev, s_cur), axis=-1, keepdims=True), sink)
            p_prev = jnp.exp(s_prev - m)
            p_cur = jnp.exp(s_cur - m)
            l = jnp.sum(p_prev + p_cur, axis=-1, keepdims=True) + jnp.exp(sink - m)
            o = (_dot(p_prev.astype(BF16), vp[e]) + _dot(p_cur.astype(BF16), vc[e])) / l
            out = o if out is None else out + o
        for c in range(SWA_CHUNKS):
            o_ref[:, col0 + c * LANES:col0 + (c + 1) * LANES] = out[c * BLOCK:(c + 1) * BLOCK].astype(o_ref.dtype)


def _t5_causal_bucket(dist):
    exact = REL_BUCKETS // 2
    d = np.maximum(dist, 0)
    log_b = exact + (np.log(np.maximum(d, 1) / exact) / np.log(REL_MAX_DIST / exact)
                     * (REL_BUCKETS - exact)).astype(np.int32)
    log_b = np.minimum(log_b, REL_BUCKETS - 1)
    return np.where(d < exact, d, log_b).astype(np.int32)


def _swa_bias_table(rel_bias):
    hq = rel_bias.shape[1]
    by_dist = rel_bias.astype(F32)[_t5_causal_bucket(np.arange(WINDOW))]
    masked = lambda width: jnp.full((hq, width), NEG_BIG, F32)
    n = 3 * BLOCK - 1
    w = jnp.concatenate([masked(BLOCK), by_dist[::-1].T, masked(BLOCK - 1)], axis=1)
    u = jnp.roll(w, -(BLOCK - 1), axis=1)
    flat = jnp.broadcast_to(u[:, None, :], (hq, BLOCK, n)).reshape(hq, BLOCK * n)
    bias = flat[:, :BLOCK * (n - 1)].reshape(hq, BLOCK, n - 1)[:, :, :2 * BLOCK]
    bias = bias.reshape(SWA_KV_HEADS, SWA_CHUNKS, 2, BLOCK, 2, BLOCK)
    bias = bias.transpose(0, 2, 4, 1, 3, 5)
    return bias.reshape(SWA_KV_HEADS, 2, 2, SWA_CHUNKS * BLOCK, BLOCK)


def _swa_attention(u, q_norm, k_norm, sinks, rel_bias, batch):
    t = u.shape[0]
    s = t // batch
    nblk = s // BLOCK
    nq_cols = SWA_Q_HEADS * SWA_HEAD_DIM
    k_block0 = nq_cols // LANES
    v_block0 = (nq_cols + SWA_KV_HEADS * SWA_HEAD_DIM) // LANES
    group_cols = SWA_CHUNKS * LANES
    gq = jnp.tile(q_norm.reshape(1, SWA_HEAD_DIM), (1, 2))
    gk = jnp.tile(k_norm.reshape(1, SWA_HEAD_DIM), (1, 2))

    def prev_map(col0):
        return lambda b, pair, n: (b * nblk + jnp.maximum(n - 1, 0), col0 + pair)

    def cur_map(col0):
        return lambda b, pair, n: (b * nblk + n, col0 + pair)

    return pl.pallas_call(
        _swa_kernel,
        grid=(batch, SWA_KV_HEADS // 2, nblk),
        in_specs=[
            pl.BlockSpec(memory_space=pltpu.SMEM),
            pl.BlockSpec((BLOCK, 2 * group_cols), lambda b, pair, n: (b * nblk + n, pair)),
            pl.BlockSpec((BLOCK, LANES), prev_map(k_block0)),
            pl.BlockSpec((BLOCK, LANES), cur_map(k_block0)),
            pl.BlockSpec((BLOCK, LANES), prev_map(v_block0)),
            pl.BlockSpec((BLOCK, LANES), cur_map(v_block0)),
            pl.BlockSpec((2, 2, 2, SWA_CHUNKS * BLOCK, BLOCK), lambda b, pair, n: (pair, 0, 0, 0, 0)),
            pl.BlockSpec((1, LANES), lambda b, pair, n: (0, 0)),
            pl.BlockSpec((1, LANES), lambda b, pair, n: (0, 0)),
        ],
        out_specs=pl.BlockSpec((BLOCK, 2 * group_cols), lambda b, pair, n: (b * nblk + n, pair)),
        out_shape=jax.ShapeDtypeStruct((t, nq_cols), BF16),
        compiler_params=_params("parallel", "parallel", "parallel"),
        name="swa_attention",
    )(sinks.astype(F32), u, u, u, u, u, _swa_bias_table(rel_bias), gq, gk)


def _fox_gate_kernel(u_ref, b_ref, o_ref):
    x = u_ref[...] + b_ref[...]
    log_f = -(jnp.maximum(-x, 0.0) + jnp.log1p(jnp.exp(-jnp.abs(x))))
    row = lax.broadcasted_iota(jnp.int32, log_f.shape, 0)
    c = log_f
    shift = 1
    while shift < c.shape[0]:
        c = c + jnp.where(row >= shift, pltpu.roll(c, shift, 0), 0.0)
        shift *= 2
    o_ref[...] = c


def _fox_gate_cumsum(u_f, b_f, batch):
    t = u_f.shape[0]
    s = t // batch
    return pl.pallas_call(
        _fox_gate_kernel,
        grid=(batch,),
        in_specs=[pl.BlockSpec((s, LANES), lambda b: (b, 0)), pl.BlockSpec((1, LANES), lambda b: (0, 0))],
        out_specs=pl.BlockSpec((s, LANES), lambda b: (b, 0)),
        out_shape=jax.ShapeDtypeStruct((t, LANES), F32),
        compiler_params=_params("parallel"),
        name="fox_gate_cumsum",
    )(u_f, b_f)


FOX_GATE_PIECES = 3


def _split_bf16_pieces(x):
    pieces = []
    for _ in range(FOX_GATE_PIECES - 1):
        piece = x.astype(BF16).astype(F32)
        pieces.append(piece)
        x = x - piece
    return pieces + [x]


def _fox_operand(x, gate_col, lane, head, is_query):
    base = (1 - head) * HALF_LANES
    out = jnp.where(lane // HALF_LANES == head, x, 0.0)
    for t, piece in enumerate(_split_bf16_pieces(gate_col)):
        gate_lane, one_lane = base + t, base + FOX_GATE_PIECES + t
        if not is_query:
            gate_lane, one_lane, piece = one_lane, gate_lane, -piece
        out = jnp.where(lane == gate_lane, piece, out)
        out = jnp.where(lane == one_lane, 1.0, out)
    return out.astype(BF16)


FOX_PAIRS_PER_STEP = 4


def _fox_attn_kernel(q_ref, k_ref, v_ref, cq_ref, ck_ref, gq_ref, gk_ref, o_ref, k_scr, *state_scr, tq):
    group = pl.program_id(1)
    i = pl.program_id(2)
    heads = [(p, e) for p in range(FOX_PAIRS_PER_STEP) for e in range(2)]
    state = [state_scr[3 * n:3 * n + 3] for n in range(len(heads))]

    def gate_col(c, p, e):
        lane = lax.broadcasted_iota(jnp.int32, c.shape, 1)
        head = 2 * (FOX_PAIRS_PER_STEP * group + p) + e
        return jnp.sum(jnp.where(lane == head, c, 0.0), axis=-1, keepdims=True) * LOG2_E

    def pair_cols(p):
        return slice(p * LANES, (p + 1) * LANES)

    @pl.when(i == 0)
    def _():
        for p in range(FOX_PAIRS_PER_STEP):
            kn = _rms_half_lanes(k_ref[:, pair_cols(p)].astype(F32), gk_ref[...])
            lane = lax.broadcasted_iota(jnp.int32, kn.shape, 1)
            for e in range(2):
                k_scr[2 * p + e] = _fox_operand(kn, gate_col(ck_ref[...], p, e), lane, e, False)

    scale = LOG2_E / math.sqrt(FOX_HEAD_DIM)
    lane = lax.broadcasted_iota(jnp.int32, (tq, LANES), 1)
    q_heads = []
    for p in range(FOX_PAIRS_PER_STEP):
        qn = _rms_half_lanes(q_ref[:, pair_cols(p)].astype(F32), gq_ref[...]) * scale
        q_heads += [_fox_operand(qn, gate_col(cq_ref[...], p, e), lane, e, True) for e in range(2)]

    for m_scr, l_scr, acc_scr in state:
        m_scr[...] = jnp.full(m_scr.shape, NEG_BIG, F32)
        l_scr[...] = jnp.zeros(l_scr.shape, F32)
        acc_scr[...] = jnp.zeros(acc_scr.shape, F32)

    def chunk(j, masked):
        rows = pl.ds(pl.multiple_of(j * tq, tq), tq)
        for n, (p, _) in enumerate(heads):
            s = _dot_nt(q_heads[n], k_scr[n, rows, :])
            if masked:
                s = _causal_mask(s)
            _online_softmax_step(s, v_ref[rows, pair_cols(p)], *state[n])

    def body(j, carry):
        chunk(j, False)
        return carry

    lax.fori_loop(0, i, body, 0)
    chunk(i, True)
    for p in range(FOX_PAIRS_PER_STEP):
        low, high = (_softmax_finish(*state[2 * p + e][1:]) for e in range(2))
        o_ref[:, pair_cols(p)] = jnp.where(lane < HALF_LANES, low, high).astype(o_ref.dtype)


def _fox_attention(u, c, q_norm, k_norm, batch):
    t = u.shape[0]
    s = t // batch
    tq = 512 if s % 512 == 0 and s > 512 else BLOCK
    nq = s // tq
    width = FOX_PAIRS_PER_STEP * LANES
    groups = FOX_WIDTH // width
    gq = jnp.tile(q_norm.reshape(1, FOX_HEAD_DIM), (1, 2))
    gk = jnp.tile(k_norm.reshape(1, FOX_HEAD_DIM), (1, 2))
    kernel = functools.partial(_fox_attn_kernel, tq=tq)
    return pl.pallas_call(
        kernel,
        grid=(batch, groups, nq),
        in_specs=[
            pl.BlockSpec((tq, width), lambda b, g, i: (b * nq + i, g)),
            pl.BlockSpec((s, width), lambda b, g, i: (b, groups + g)),
            pl.BlockSpec((s, width), lambda b, g, i: (b, 2 * groups + g)),
            pl.BlockSpec((tq, LANES), lambda b, g, i: (b * nq + i, 0)),
            pl.BlockSpec((s, LANES), lambda b, g, i: (b, 0)),
            pl.BlockSpec((1, LANES), lambda b, g, i: (0, 0)),
            pl.BlockSpec((1, LANES), lambda b, g, i: (0, 0)),
        ],
        out_specs=pl.BlockSpec((tq, width), lambda b, g, i: (b * nq + i, g)),
        out_shape=jax.ShapeDtypeStruct((t, FOX_WIDTH), BF16),
        scratch_shapes=[pltpu.VMEM((2 * FOX_PAIRS_PER_STEP, s, LANES), BF16)] + 2 * FOX_PAIRS_PER_STEP * [
            pltpu.VMEM((tq, LANES), F32),
            pltpu.VMEM((tq, LANES), F32),
            pltpu.VMEM((tq, LANES), F32),
        ],
        compiler_params=_params("parallel", "parallel", "arbitrary"),
        name="fox_attention",
    )(u, u, u, c, c, gq, gk)


def kernel(x, mem, norm_ffn1, ffn1_w_gate, ffn1_w_up, ffn1_w_down, norm_mix, norm_ffn2, ffn2_w_gate, ffn2_w_up, ffn2_w_down, norm_mem, mem_w_kv, mem_q_norm, mem_k_norm, conv_w_in, conv_w, conv_w_out, mla_w_in, mla_q_a_norm, mla_w_q_b, mla_kv_a_norm, mla_w_kv_b, mla_q_norm, mla_k_norm, mla_w_out, swa_w_in, swa_q_norm, swa_k_norm, swa_sinks, swa_w_out, rel_bias, fox_w_in, fox_b_f, fox_q_norm, fox_k_norm, fox_w_out):
    b, s, d = x.shape
    m_len = mem.shape[1]
    depth = norm_ffn1.shape[0]
    xt = x.reshape(b * s, d)
    mem2 = mem.reshape(b * m_len, d)
    bf = lambda w: w.astype(BF16)
    ffn1_w = (bf(ffn1_w_gate), bf(ffn1_w_up), bf(ffn1_w_down))
    ffn2_w = (bf(ffn2_w_gate), bf(ffn2_w_up), bf(ffn2_w_down))

    for i in range(depth):
        kind, occ = i % 4, i // 4
        xt = _ffn(xt, norm_ffn1, *ffn1_w, i)

        if kind == 0:
            c = conv_w.shape[-1]
            u = _norm_matmul(xt, norm_mix[i], bf(conv_w_in[occ]), name="conv_in_proj")
            memq_block = 3 * c // MEM_WIDTH
            mix = _conv_mixer(u, conv_w[occ], b)
            w_out = conv_w_out[occ]
        elif kind == 1:
            w_in = bf(mla_w_in[occ])
            lat = MLA_Q_RANK + MLA_KV_RANK
            w_in = jnp.concatenate([w_in[:, :lat], w_in[:, lat + MLA_ROPE:], w_in[:, lat:lat + MLA_ROPE],
                                    jnp.zeros((d, LANES - MLA_ROPE), BF16)], axis=1)
            u = _norm_matmul(xt, norm_mix[i], w_in, name="mla_in_proj")
            memq_block = lat // MEM_WIDTH
            krope_block = (lat + MEM_WIDTH) // LANES
            wq = mla_w_q_b[occ].reshape(MLA_Q_RANK, MLA_HEADS, MLA_QK)
            wq = jnp.concatenate([wq, jnp.zeros((MLA_Q_RANK, MLA_HEADS, MLA_QPAD - MLA_QK), F32)], axis=-1)
            wkv = mla_w_kv_b[occ].reshape(MLA_KV_RANK, MLA_HEADS, MLA_NOPE + MLA_V)
            wkv = jnp.concatenate([wkv[..., :MLA_NOPE].reshape(MLA_KV_RANK, -1),
                                   wkv[..., MLA_NOPE:].reshape(MLA_KV_RANK, -1)], axis=1)
            q = _norm_matmul(u, mla_q_a_norm[occ], bf(wq.reshape(MLA_Q_RANK, -1)), col_block=0, name="mla_q_proj")
            kv = _norm_matmul(u, mla_kv_a_norm[occ], bf(wkv), col_block=1, name="mla_kv_proj")
            pad = jnp.zeros((MLA_QPAD - MLA_QK,), F32)
            gq = jnp.concatenate([mla_q_norm[occ], pad]).reshape(1, MLA_QPAD)
            gk = jnp.concatenate([mla_k_norm[occ], pad]).reshape(1, MLA_QPAD)
            cos_t, sin_t = _rope_tables(s)
            mix = _mla_attention(q, kv, u, krope_block, cos_t, sin_t, gq, gk, b)
            w_out = mla_w_out[occ]
        elif kind == 2:
            u = _norm_matmul(xt, norm_mix[i], bf(swa_w_in[occ]), name="swa_in_proj")
            memq_block = (SWA_Q_HEADS + 2 * SWA_KV_HEADS) * SWA_HEAD_DIM // MEM_WIDTH
            mix = _swa_attention(u, swa_q_norm[occ], swa_k_norm[occ], swa_sinks[occ], rel_bias, b)
            w_out = swa_w_out[occ]
        else:
            w_in = bf(fox_w_in[occ])
            qkv = 3 * FOX_WIDTH
            w_main = jnp.concatenate([w_in[:, :qkv], w_in[:, qkv + FOX_HEADS:]], axis=1)
            w_gate = jnp.concatenate([w_in[:, qkv:qkv + FOX_HEADS], jnp.zeros((d, LANES - FOX_HEADS), BF16)], axis=1)
            u = _norm_matmul(xt, norm_mix[i], w_main, name="fox_in_proj")
            u_f = _norm_matmul(xt, norm_mix[i], w_gate, out_dtype=F32, name="fox_gate_proj")
            b_f = jnp.concatenate([fox_b_f[occ], jnp.zeros((LANES - FOX_HEADS,), F32)]).reshape(1, LANES)
            c_gate = _fox_gate_cumsum(u_f, b_f, b)
            memq_block = qkv // MEM_WIDTH
            mix = _fox_attention(u, c_gate, fox_q_norm[occ], fox_k_norm[occ], b)
            w_out = fox_w_out[occ]

        mkv = _norm_matmul(mem2, norm_mem[i], bf(mem_w_kv[i]), name="mem_kv_proj")
        mem_out = _mem_attn(u, memq_block, mkv, mem_q_norm[i], mem_k_norm[i], b)
        k_mix = mix.shape[1]
        xt = _out_proj(mix, mem_out, bf(w_out[:k_mix]), bf(w_out[k_mix:]), xt)

        xt = _ffn(xt, norm_ffn2, *ffn2_w, i)
    return xt.reshape(b, s, d)
```

```python
import functools
import math

import numpy as np
import jax
import jax.numpy as jnp
from jax import lax
from jax.experimental import pallas as pl
from jax.experimental.pallas import tpu as pltpu

F32 = jnp.float32
BF16 = jnp.bfloat16

RMS_EPS = 1e-6
NEG_BIG = -1e30
LOG2_E = math.log2(math.e)
LANES = 128
HALF_LANES = LANES // 2
VMEM_LIMIT_BYTES = 56 * 1024 * 1024

BLOCK = 128
MEM_HEADS = 4
MEM_HEAD_DIM = 128
MEM_WIDTH = MEM_HEADS * MEM_HEAD_DIM
CONV_K = 3
MLA_HEADS = 16
MLA_Q_RANK = 512
MLA_KV_RANK = 512
MLA_NOPE = 128
MLA_ROPE = 64
MLA_V = 128
MLA_QK = MLA_NOPE + MLA_ROPE
MLA_QPAD = 2 * LANES
ROPE_THETA = 10000.0
SWA_Q_HEADS = 32
SWA_KV_HEADS = 4
SWA_GROUP = SWA_Q_HEADS // SWA_KV_HEADS
SWA_HEAD_DIM = 64
WINDOW = 128
REL_BUCKETS = 32
REL_MAX_DIST = 128
FOX_HEADS = 32
FOX_HEAD_DIM = 64
FOX_WIDTH = FOX_HEADS * FOX_HEAD_DIM


def _params(*semantics):
    return pltpu.CompilerParams(dimension_semantics=semantics, vmem_limit_bytes=VMEM_LIMIT_BYTES)


def _pick_tile(n, target):
    best = None
    for t in range(LANES, min(n, target) + 1, LANES):
        if n % t == 0:
            best = t
    return n if best is None else best


def _rms_rows(x, gain):
    r = lax.rsqrt(jnp.mean(x * x, axis=-1, keepdims=True) + RMS_EPS)
    return x * r * gain


def _rms_half_lanes(x, gain):
    lane = lax.broadcasted_iota(jnp.int32, x.shape, 1)
    low = lane < HALF_LANES
    sq = x * x
    s_low = jnp.sum(jnp.where(low, sq, 0.0), axis=-1, keepdims=True)
    s_high = jnp.sum(jnp.where(low, 0.0, sq), axis=-1, keepdims=True)
    r = lax.rsqrt(jnp.where(low, s_low, s_high) * (1.0 / HALF_LANES) + RMS_EPS)
    return x * r * gain


def _dot(a, b):
    return jnp.dot(a, b, preferred_element_type=F32)


def _dot_nt(a, b):
    return lax.dot_general(a, b, (((1,), (1,)), ((), ())), preferred_element_type=F32)


FFN_ROWS = 1024
FFN_SPLIT = 2


def _ffn_kernel(x_ref, g_ref, wg_ref, wu_ref, wd_ref, o_ref, h_scr):
    @pl.when(pl.program_id(1) == 0)
    def _():
        x = x_ref[...]
        h_scr[...] = _rms_rows(x, g_ref[...]).astype(BF16)
        o_ref[...] = x

    h = h_scr[...]
    width = wg_ref.shape[-1] // FFN_SPLIT
    acts = []
    for c in range(FFN_SPLIT):
        cols = slice(c * width, (c + 1) * width)
        gate = _dot(h, wg_ref[:, cols])
        up = _dot(h, wu_ref[:, cols])
        acts.append((gate / (1.0 + jnp.exp(-gate)) * (0.5 * up)).astype(BF16))
    o_ref[...] += _dot(jnp.concatenate(acts, axis=-1), wd_ref[...])


def _ffn(x, gain, w_gate, w_up, w_down, layer):
    t, d = x.shape
    f = w_gate.shape[-1]
    tm = _pick_tile(t, FFN_ROWS)
    tf = _pick_tile(f, 512)
    return pl.pallas_call(
        _ffn_kernel,
        grid=(t // tm, f // tf),
        in_specs=[
            pl.BlockSpec((tm, d), lambda i, j: (i, 0)),
            pl.BlockSpec((None, 1, d), lambda i, j: (layer, 0, 0)),
            pl.BlockSpec((None, d, tf), lambda i, j: (layer, 0, j)),
            pl.BlockSpec((None, d, tf), lambda i, j: (layer, 0, j)),
            pl.BlockSpec((None, tf, d), lambda i, j: (layer, j, 0)),
        ],
        out_specs=pl.BlockSpec((tm, d), lambda i, j: (i, 0)),
        out_shape=jax.ShapeDtypeStruct((t, d), F32),
        scratch_shapes=[pltpu.VMEM((tm, d), BF16)],
        compiler_params=_params("parallel", "arbitrary"),
        name="ffn",
    )(x, gain.reshape(gain.shape[0], 1, d), w_gate, w_up, w_down)


NORM_MATMUL_COLS_LONG_K = 1792
NORM_MATMUL_COLS_SHORT_K = 2048


def _norm_matmul_kernel(x_ref, g_ref, w_ref, o_ref, h_scr):
    @pl.when(pl.program_id(1) == 0)
    def _():
        h_scr[...] = _rms_rows(x_ref[...].astype(F32), g_ref[...]).astype(BF16)

    o_ref[...] = _dot(h_scr[...], w_ref[...]).astype(o_ref.dtype)


def _norm_matmul(x, gain, w, *, col_block=0, out_dtype=BF16, name="norm_matmul"):
    m = x.shape[0]
    k, n = w.shape
    tm = _pick_tile(m, 1024)
    tn = n if n <= 2048 else _pick_tile(n, NORM_MATMUL_COLS_LONG_K if k > 1024 else NORM_MATMUL_COLS_SHORT_K)
    return pl.pallas_call(
        _norm_matmul_kernel,
        grid=(m // tm, n // tn),
        in_specs=[
            pl.BlockSpec((tm, k), lambda i, j: (i, col_block)),
            pl.BlockSpec((1, k), lambda i, j: (0, 0)),
            pl.BlockSpec((k, tn), lambda i, j: (0, j)),
        ],
        out_specs=pl.BlockSpec((tm, tn), lambda i, j: (i, j)),
        out_shape=jax.ShapeDtypeStruct((m, n), out_dtype),
        scratch_shapes=[pltpu.VMEM((tm, k), BF16)],
        compiler_params=_params("parallel", "arbitrary"),
        name=name,
    )(x, gain.reshape(1, k), w)


def _out_proj_kernel(a_ref, b_ref, wa_ref, wb_ref, x_ref, o_ref):
    acc = _dot(a_ref[...], wa_ref[...])
    acc = acc + _dot(b_ref[...], wb_ref[...])
    o_ref[...] = x_ref[...] + acc


def _out_proj(mix, mem_out, w_mix, w_mem, x):
    t, d = x.shape
    ka = mix.shape[1]
    kb = mem_out.shape[1]
    tm = _pick_tile(t, 512)
    tn = _pick_tile(d, 2048)
    return pl.pallas_call(
        _out_proj_kernel,
        grid=(t // tm, d // tn),
        in_specs=[
            pl.BlockSpec((tm, ka), lambda i, j: (i, 0)),
            pl.BlockSpec((tm, kb), lambda i, j: (i, 0)),
            pl.BlockSpec((ka, tn), lambda i, j: (0, j)),
            pl.BlockSpec((kb, tn), lambda i, j: (0, j)),
            pl.BlockSpec((tm, tn), lambda i, j: (i, j)),
        ],
        out_specs=pl.BlockSpec((tm, tn), lambda i, j: (i, j)),
        out_shape=jax.ShapeDtypeStruct((t, d), F32),
        compiler_params=_params("parallel", "parallel"),
        name="out_proj",
    )(mix, mem_out, w_mix, w_mem, x)


def _mem_attn_kernel(q_ref, kv_ref, gq_ref, gk_ref, o_ref):
    scale = 1.0 / math.sqrt(MEM_HEAD_DIM)
    for h in range(MEM_HEADS):
        cols = slice(h * MEM_HEAD_DIM, (h + 1) * MEM_HEAD_DIM)
        vcols = slice(MEM_WIDTH + h * MEM_HEAD_DIM, MEM_WIDTH + (h + 1) * MEM_HEAD_DIM)
        q = (_rms_rows(q_ref[:, cols].astype(F32), gq_ref[...]) * scale).astype(BF16)
        k = _rms_rows(kv_ref[:, cols].astype(F32), gk_ref[...]).astype(BF16)
        s = _dot_nt(q, k)
        m = jnp.max(s, axis=-1, keepdims=True)
        p = jnp.exp(s - m)
        l = jnp.sum(p, axis=-1, keepdims=True)
        o = _dot(p.astype(BF16), kv_ref[:, vcols]) / l
        o_ref[:, cols] = o.astype(o_ref.dtype)


def _mem_attn(u, q_col_block, mkv, gq, gk, batch):
    t = u.shape[0]
    s = t // batch
    m_len = mkv.shape[0] // batch
    tq = _pick_tile(s, 512)
    nq = s // tq
    return pl.pallas_call(
        _mem_attn_kernel,
        grid=(batch, nq),
        in_specs=[
            pl.BlockSpec((tq, MEM_WIDTH), lambda b, i: (b * nq + i, q_col_block)),
            pl.BlockSpec((m_len, 2 * MEM_WIDTH), lambda b, i: (b, 0)),
            pl.BlockSpec((1, MEM_HEAD_DIM), lambda b, i: (0, 0)),
            pl.BlockSpec((1, MEM_HEAD_DIM), lambda b, i: (0, 0)),
        ],
        out_specs=pl.BlockSpec((tq, MEM_WIDTH), lambda b, i: (b * nq + i, 0)),
        out_shape=jax.ShapeDtypeStruct((t, MEM_WIDTH), BF16),
        compiler_params=_params("parallel", "parallel"),
        name="mem_attn",
    )(u, mkv, gq.reshape(1, MEM_HEAD_DIM), gk.reshape(1, MEM_HEAD_DIM))


CONV_HALO = 16


def _conv_kernel(gb_ref, gc_ref, xt_ref, gch_ref, xth_ref, w_ref, o_ref, *, tiles_per_seq):
    i = pl.program_id(0)
    z = gc_ref[...].astype(F32) * xt_ref[...].astype(F32)
    zh = gch_ref[...].astype(F32) * xth_ref[...].astype(F32)
    zh = jnp.where(i % tiles_per_seq == 0, 0.0, zh)
    zc = jnp.concatenate([zh[CONV_HALO - 8:], z], axis=0)
    z1 = pltpu.roll(zc, 1, 0)[8:]
    z2 = pltpu.roll(zc, 2, 0)[8:]
    w = w_ref[...]
    conv = z2 * w[0:1] + z1 * w[1:2] + z * w[2:3]
    o_ref[...] = (gb_ref[...].astype(F32) * conv).astype(o_ref.dtype)


def _conv_mixer(u, conv_w, batch):
    t = u.shape[0]
    s = t // batch
    c = conv_w.shape[1]
    ts = _pick_tile(s, 512)
    tc = _pick_tile(c, 512)
    nc = c // tc
    halo_per_tile = ts // CONV_HALO

    def halo_map(col0):
        return lambda i, j: (jnp.maximum(i * halo_per_tile - 1, 0), col0 + j)

    return pl.pallas_call(
        functools.partial(_conv_kernel, tiles_per_seq=s // ts),
        grid=(t // ts, nc),
        in_specs=[
            pl.BlockSpec((ts, tc), lambda i, j: (i, j)),
            pl.BlockSpec((ts, tc), lambda i, j: (i, nc + j)),
            pl.BlockSpec((ts, tc), lambda i, j: (i, 2 * nc + j)),
            pl.BlockSpec((CONV_HALO, tc), halo_map(nc)),
            pl.BlockSpec((CONV_HALO, tc), halo_map(2 * nc)),
            pl.BlockSpec((CONV_K, tc), lambda i, j: (0, j)),
        ],
        out_specs=pl.BlockSpec((ts, tc), lambda i, j: (i, j)),
        out_shape=jax.ShapeDtypeStruct((t, c), BF16),
        compiler_params=_params("parallel", "parallel"),
        name="conv_mixer",
    )(u, u, u, u, u, conv_w)


def _online_softmax_step(s, v, m_ref, l_ref, acc_ref):
    m_prev = m_ref[...]
    m_new = jnp.maximum(m_prev, jnp.max(s, axis=-1, keepdims=True))
    alpha = jnp.exp2(m_prev - m_new)
    p = [jnp.exp2(s[:, t * LANES:(t + 1) * LANES] - m_new) for t in range(s.shape[1] // LANES)]
    l_ref[...] = alpha * l_ref[...] + functools.reduce(lambda a, b: a + b, p)
    acc_ref[...] = alpha * acc_ref[...] + _dot(jnp.concatenate(p, axis=-1).astype(BF16), v)
    m_ref[...] = m_new


def _softmax_finish(l_ref, acc_ref):
    return acc_ref[...] / jnp.sum(l_ref[...], axis=-1, keepdims=True)


def _causal_mask(s):
    row = lax.broadcasted_iota(jnp.int32, s.shape, 0)
    col = lax.broadcasted_iota(jnp.int32, s.shape, 1)
    return jnp.where(col <= row, s, NEG_BIG)


def _swap_rope_halves(t):
    lane = lax.broadcasted_iota(jnp.int32, t.shape, 1)
    up = pltpu.roll(t, MLA_ROPE // 2, 1)
    down = pltpu.roll(t, LANES - MLA_ROPE // 2, 1)
    return jnp.where(lane < MLA_ROPE // 2, down, up)


def _rope_lanes(t, cos_t, sin_t):
    return t * cos_t + _swap_rope_halves(t) * sin_t


def _rms_rope_part(x, gain):
    r = lax.rsqrt(jnp.sum(x * x, axis=-1, keepdims=True) * (1.0 / MLA_ROPE) + RMS_EPS)
    return x * r * gain


MLA_HEADS_PER_STEP = 8


def _mla_attn_kernel(q_ref, kn_ref, v_ref, kr_ref, cosq_ref, sinq_ref, cosk_ref, sink_ref,
                     gq_ref, gk_ref, o_ref, k_scr, *state_scr, tq):
    i = pl.program_id(2)
    state = [state_scr[3 * e:3 * e + 3] for e in range(MLA_HEADS_PER_STEP)]

    @pl.when(i == 0)
    def _():
        kr = _rms_rope_part(kr_ref[...].astype(F32), gk_ref[:, LANES:])
        kr = _rope_lanes(kr, cosk_ref[...], sink_ref[...]).astype(BF16)
        for e in range(MLA_HEADS_PER_STEP):
            kn = kn_ref[:, e * LANES:(e + 1) * LANES].astype(F32)
            k_scr[e, :, :LANES] = _rms_rows(kn, gk_ref[:, :LANES]).astype(BF16)
            k_scr[e, :, LANES:] = kr

    scale = LOG2_E / math.sqrt(MLA_QK)
    q_heads = []
    for e in range(MLA_HEADS_PER_STEP):
        c0 = e * MLA_QPAD
        q_nope = _rms_rows(q_ref[:, c0:c0 + LANES].astype(F32), gq_ref[:, :LANES])
        q_rot = _rms_rope_part(q_ref[:, c0 + LANES:c0 + MLA_QPAD].astype(F32), gq_ref[:, LANES:])
        q_rot = _rope_lanes(q_rot, cosq_ref[...], sinq_ref[...])
        q_heads.append((jnp.concatenate([q_nope, q_rot], axis=-1) * scale).astype(BF16))

    for m_scr, l_scr, acc_scr in state:
        m_scr[...] = jnp.full(m_scr.shape, NEG_BIG, F32)
        l_scr[...] = jnp.zeros(l_scr.shape, F32)
        acc_scr[...] = jnp.zeros(acc_scr.shape, F32)

    def chunk(j, masked):
        rows = pl.ds(pl.multiple_of(j * tq, tq), tq)
        for e in range(MLA_HEADS_PER_STEP):
            s = _dot_nt(q_heads[e], k_scr[e, rows, :])
            if masked:
                s = _causal_mask(s)
            _online_softmax_step(s, v_ref[rows, e * MLA_V:(e + 1) * MLA_V], *state[e])

    def body(j, carry):
        chunk(j, False)
        return carry

    lax.fori_loop(0, i, body, 0)
    chunk(i, True)
    for e, (_, l_scr, acc_scr) in enumerate(state):
        o_ref[:, e * MLA_V:(e + 1) * MLA_V] = _softmax_finish(l_scr, acc_scr).astype(o_ref.dtype)


def _mla_attention(q, kv, u, krope_block, cos_t, sin_t, gq, gk, batch):
    t = q.shape[0]
    s = t // batch
    tq = 512 if s % 512 == 0 and s > 512 else BLOCK
    nq = s // tq
    h = MLA_HEADS
    hs = MLA_HEADS_PER_STEP
    groups = h // hs
    kernel = functools.partial(_mla_attn_kernel, tq=tq)
    return pl.pallas_call(
        kernel,
        grid=(batch, groups, nq),
        in_specs=[
            pl.BlockSpec((tq, hs * MLA_QPAD), lambda b, hh, i: (b * nq + i, hh)),
            pl.BlockSpec((s, hs * LANES), lambda b, hh, i: (b, hh)),
            pl.BlockSpec((s, hs * LANES), lambda b, hh, i: (b, groups + hh)),
            pl.BlockSpec((s, LANES), lambda b, hh, i: (b, krope_block)),
            pl.BlockSpec((tq, LANES), lambda b, hh, i: (i, 0)),
            pl.BlockSpec((tq, LANES), lambda b, hh, i: (i, 0)),
            pl.BlockSpec((s, LANES), lambda b, hh, i: (0, 0)),
            pl.BlockSpec((s, LANES), lambda b, hh, i: (0, 0)),
            pl.BlockSpec((1, MLA_QPAD), lambda b, hh, i: (0, 0)),
            pl.BlockSpec((1, MLA_QPAD), lambda b, hh, i: (0, 0)),
        ],
        out_specs=pl.BlockSpec((tq, hs * MLA_V), lambda b, hh, i: (b * nq + i, hh)),
        out_shape=jax.ShapeDtypeStruct((t, h * MLA_V), BF16),
        scratch_shapes=[pltpu.VMEM((hs, s, MLA_QPAD), BF16)] + hs * [
            pltpu.VMEM((tq, LANES), F32),
            pltpu.VMEM((tq, LANES), F32),
            pltpu.VMEM((tq, MLA_V), F32),
        ],
        compiler_params=_params("parallel", "parallel", "arbitrary"),
        name="mla_attention",
    )(q, kv, kv, u, cos_t, sin_t, cos_t, sin_t, gq, gk)


def _rope_tables(s):
    half = MLA_ROPE // 2
    inv = ROPE_THETA ** (-jnp.arange(half, dtype=F32) / half)
    ang = jnp.arange(s).astype(F32)[:, None] * inv
    cos, sin = jnp.cos(ang), jnp.sin(ang)
    zeros = jnp.zeros((s, LANES - MLA_ROPE), F32)
    return (jnp.concatenate([cos, cos, zeros], axis=-1), jnp.concatenate([-sin, sin, zeros], axis=-1))


SWA_CHUNKS = SWA_GROUP * SWA_HEAD_DIM // LANES


def _swa_kernel(sinks_ref, q_ref, kp_ref, kc_ref, vp_ref, vc_ref, bias_ref, gq_ref, gk_ref, o_ref):
    n = pl.program_id(1)
    lane = lax.broadcasted_iota(jnp.int32, (BLOCK, LANES), 1)
    scale = 1.0 / math.sqrt(SWA_HEAD_DIM)
    group_cols = SWA_CHUNKS * LANES
    row_chunk = lax.broadcasted_iota(jnp.int32, (SWA_CHUNKS * BLOCK, 1), 0) // BLOCK

    for kh in range(SWA_KV_HEADS):
        pair, half = divmod(kh, 2)
        pair_cols = slice(pair * LANES, (pair + 1) * LANES)
        if half == 0:
            kp_n = _rms_half_lanes(kp_ref[:, pair_cols].astype(F32), gk_ref[...])
            kc_n = _rms_half_lanes(kc_ref[:, pair_cols].astype(F32), gk_ref[...])
            vp_f = vp_ref[:, pair_cols].astype(F32)
            vc_f = vc_ref[:, pair_cols].astype(F32)
        own = lane // HALF_LANES == half
        col0 = kh * group_cols

        def by_parity(x):
            mine = jnp.where(own, x, 0.0)
            other = pltpu.roll(mine, HALF_LANES, 1)
            low, high = (mine, other) if half == 0 else (other, mine)
            return low.astype(BF16), high.astype(BF16)

        kp, kc, vp, vc = by_parity(kp_n), by_parity(kc_n), by_parity(vp_f), by_parity(vc_f)
        q = jnp.concatenate(
            [_rms_half_lanes(q_ref[:, col0 + c * LANES:col0 + (c + 1) * LANES].astype(F32), gq_ref[...]) * scale
             for c in range(SWA_CHUNKS)], axis=0).astype(BF16)

        out = None
        for e in range(2):
            sink = jnp.zeros((SWA_CHUNKS * BLOCK, 1), F32)
            for c in range(SWA_CHUNKS):
                sink = jnp.where(row_chunk == c, sinks_ref[kh * SWA_GROUP + 2 * c + e], sink)
            s_prev = jnp.where(n > 0, _dot_nt(q, kp[e]) + bias_ref[kh, e, 0], NEG_BIG)
            s_cur = _dot_nt(q, kc[e]) + bias_ref[kh, e, 1]
            m = jnp.maximum(jnp.max(jnp.maximum(s_prev, s_cur), axis=-1, keepdims=True), sink)
            p_prev = jnp.exp(s_prev - m)
            p_cur = jnp.exp(s_cur - m)
            l = jnp.sum(p_prev + p_cur, axis=-1, keepdims=True) + jnp.exp(sink - m)
            o = (_dot(p_prev.astype(BF16), vp[e]) + _dot(p_cur.astype(BF16), vc[e])) / l
            out = o if out is None else out + o
        for c in range(SWA_CHUNKS):
            o_ref[:, col0 + c * LANES:col0 + (c + 1) * LANES] = out[c * BLOCK:(c + 1) * BLOCK].astype(o_ref.dtype)


def _t5_causal_bucket(dist):
    exact = REL_BUCKETS // 2
    d = np.maximum(dist, 0)
    log_b = exact + (np.log(np.maximum(d, 1) / exact) / np.log(REL_MAX_DIST / exact)
                     * (REL_BUCKETS - exact)).astype(np.int32)
    log_b = np.minimum(log_b, REL_BUCKETS - 1)
    return np.where(d < exact, d, log_b).astype(np.int32)


def _swa_bias_table(rel_bias):
    hq = rel_bias.shape[1]
    by_dist = rel_bias.astype(F32)[_t5_causal_bucket(np.arange(WINDOW))]
    masked = lambda width: jnp.full((hq, width), NEG_BIG, F32)
    n = 3 * BLOCK - 1
    w = jnp.concatenate([masked(BLOCK), by_dist[::-1].T, masked(BLOCK - 1)], axis=1)
    u = jnp.roll(w, -(BLOCK - 1), axis=1)
    flat = jnp.broadcast_to(u[:, None, :], (hq, BLOCK, n)).reshape(hq, BLOCK * n)
    bias = flat[:, :BLOCK * (n - 1)].reshape(hq, BLOCK, n - 1)[:, :, :2 * BLOCK]
    bias = bias.reshape(SWA_KV_HEADS, SWA_CHUNKS, 2, BLOCK, 2, BLOCK)
    bias = bias.transpose(0, 2, 4, 1, 3, 5)
    return bias.reshape(SWA_KV_HEADS, 2, 2, SWA_CHUNKS * BLOCK, BLOCK)


def _swa_attention(u, q_norm, k_norm, sinks, rel_bias, batch):
    t = u.shape[0]
    s = t // batch
    nblk = s // BLOCK
    nq_cols = SWA_Q_HEADS * SWA_HEAD_DIM
    gq = jnp.tile(q_norm.reshape(1, SWA_HEAD_DIM), (1, 2))
    gk = jnp.tile(k_norm.reshape(1, SWA_HEAD_DIM), (1, 2))

    kv_cols = SWA_KV_HEADS * SWA_HEAD_DIM

    def prev_map(col0):
        return lambda b, n: (b * nblk + jnp.maximum(n - 1, 0), col0 // kv_cols)

    def cur_map(col0):
        return lambda b, n: (b * nblk + n, col0 // kv_cols)

    return pl.pallas_call(
        _swa_kernel,
        grid=(batch, nblk),
        in_specs=[
            pl.BlockSpec(memory_space=pltpu.SMEM),
            pl.BlockSpec((BLOCK, nq_cols), lambda b, n: (b * nblk + n, 0)),
            pl.BlockSpec((BLOCK, kv_cols), prev_map(nq_cols)),
            pl.BlockSpec((BLOCK, kv_cols), cur_map(nq_cols)),
            pl.BlockSpec((BLOCK, kv_cols), prev_map(nq_cols + kv_cols)),
            pl.BlockSpec((BLOCK, kv_cols), cur_map(nq_cols + kv_cols)),
            pl.BlockSpec((SWA_KV_HEADS, 2, 2, SWA_CHUNKS * BLOCK, BLOCK), lambda b, n: (0, 0, 0, 0, 0)),
            pl.BlockSpec((1, LANES), lambda b, n: (0, 0)),
            pl.BlockSpec((1, LANES), lambda b, n: (0, 0)),
        ],
        out_specs=pl.BlockSpec((BLOCK, nq_cols), lambda b, n: (b * nblk + n, 0)),
        out_shape=jax.ShapeDtypeStruct((t, nq_cols), BF16),
        compiler_params=_params("parallel", "parallel"),
        name="swa_attention",
    )(sinks.astype(F32), u, u, u, u, u, _swa_bias_table(rel_bias), gq, gk)


def _fox_gate_kernel(u_ref, b_ref, o_ref):
    x = u_ref[...] + b_ref[...]
    log_f = -(jnp.maximum(-x, 0.0) + jnp.log1p(jnp.exp(-jnp.abs(x))))
    row = lax.broadcasted_iota(jnp.int32, log_f.shape, 0)
    c = log_f
    shift = 1
    while shift < c.shape[0]:
        c = c + jnp.where(row >= shift, pltpu.roll(c, shift, 0), 0.0)
        shift *= 2
    o_ref[...] = c


def _fox_gate_cumsum(u_f, b_f, batch):
    t = u_f.shape[0]
    s = t // batch
    return pl.pallas_call(
        _fox_gate_kernel,
        grid=(batch,),
        in_specs=[pl.BlockSpec((s, LANES), lambda b: (b, 0)), pl.BlockSpec((1, LANES), lambda b: (0, 0))],
        out_specs=pl.BlockSpec((s, LANES), lambda b: (b, 0)),
        out_shape=jax.ShapeDtypeStruct((t, LANES), F32),
        compiler_params=_params("parallel"),
        name="fox_gate_cumsum",
    )(u_f, b_f)


FOX_GATE_PIECES = 3


def _split_bf16_pieces(x):
    pieces = []
    for _ in range(FOX_GATE_PIECES - 1):
        piece = x.astype(BF16).astype(F32)
        pieces.append(piece)
        x = x - piece
    return pieces + [x]


def _fox_operand(x, gate_col, lane, head, is_query):
    base = (1 - head) * HALF_LANES
    out = jnp.where(lane // HALF_LANES == head, x, 0.0)
    for t, piece in enumerate(_split_bf16_pieces(gate_col)):
        gate_lane, one_lane = base + t, base + FOX_GATE_PIECES + t
        if not is_query:
            gate_lane, one_lane, piece = one_lane, gate_lane, -piece
        out = jnp.where(lane == gate_lane, piece, out)
        out = jnp.where(lane == one_lane, 1.0, out)
    return out.astype(BF16)


FOX_PAIRS_PER_STEP = 4


def _fox_attn_kernel(q_ref, k_ref, v_ref, cq_ref, ck_ref, gq_ref, gk_ref, o_ref, k_scr, *state_scr, tq):
    group = pl.program_id(1)
    i = pl.program_id(2)
    heads = [(p, e) for p in range(FOX_PAIRS_PER_STEP) for e in range(2)]
    state = [state_scr[3 * n:3 * n + 3] for n in range(len(heads))]

    def gate_col(c, p, e):
        lane = lax.broadcasted_iota(jnp.int32, c.shape, 1)
        head = 2 * (FOX_PAIRS_PER_STEP * group + p) + e
        return jnp.sum(jnp.where(lane == head, c, 0.0), axis=-1, keepdims=True) * LOG2_E

    def pair_cols(p):
        return slice(p * LANES, (p + 1) * LANES)

    @pl.when(i == 0)
    def _():
        for p in range(FOX_PAIRS_PER_STEP):
            kn = _rms_half_lanes(k_ref[:, pair_cols(p)].astype(F32), gk_ref[...])
            lane = lax.broadcasted_iota(jnp.int32, kn.shape, 1)
            for e in range(2):
                k_scr[2 * p + e] = _fox_operand(kn, gate_col(ck_ref[...], p, e), lane, e, False)

    scale = LOG2_E / math.sqrt(FOX_HEAD_DIM)
    lane = lax.broadcasted_iota(jnp.int32, (tq, LANES), 1)
    q_heads = []
    for p in range(FOX_PAIRS_PER_STEP):
        qn = _rms_half_lanes(q_ref[:, pair_cols(p)].astype(F32), gq_ref[...]) * scale
        q_heads += [_fox_operand(qn, gate_col(cq_ref[...], p, e), lane, e, True) for e in range(2)]

    for m_scr, l_scr, acc_scr in state:
        m_scr[...] = jnp.full(m_scr.shape, NEG_BIG, F32)
        l_scr[...] = jnp.zeros(l_scr.shape, F32)
        acc_scr[...] = jnp.zeros(acc_scr.shape, F32)

    def chunk(j, masked):
        rows = pl.ds(pl.multiple_of(j * tq, tq), tq)
        for n, (p, _) in enumerate(heads):
            s = _dot_nt(q_heads[n], k_scr[n, rows, :])
            if masked:
                s = _causal_mask(s)
            _online_softmax_step(s, v_ref[rows, pair_cols(p)], *state[n])

    def body(j, carry):
        chunk(j, False)
        return carry

    lax.fori_loop(0, i, body, 0)
    chunk(i, True)
    for p in range(FOX_PAIRS_PER_STEP):
        low, high = (_softmax_finish(*state[2 * p + e][1:]) for e in range(2))
        o_ref[:, pair_cols(p)] = jnp.where(lane < HALF_LANES, low, high).astype(o_ref.dtype)


def _fox_attention(u, c, q_norm, k_norm, batch):
    t = u.shape[0]
    s = t // batch
    tq = 512 if s % 512 == 0 and s > 512 else BLOCK
    nq = s // tq
    width = FOX_PAIRS_PER_STEP * LANES
    groups = FOX_WIDTH // width
    gq = jnp.tile(q_norm.reshape(1, FOX_HEAD_DIM), (1, 2))
    gk = jnp.tile(k_norm.reshape(1, FOX_HEAD_DIM), (1, 2))
    kernel = functools.partial(_fox_attn_kernel, tq=tq)
    return pl.pallas_call(
        kernel,
        grid=(batch, groups, nq),
        in_specs=[
            pl.BlockSpec((tq, width), lambda b, g, i: (b * nq + i, g)),
            pl.BlockSpec((s, width), lambda b, g, i: (b, groups + g)),
            pl.BlockSpec((s, width), lambda b, g, i: (b, 2 * groups + g)),
            pl.BlockSpec((tq, LANES), lambda b, g, i: (b * nq + i, 0)),
            pl.BlockSpec((s, LANES), lambda b, g, i: (b, 0)),
            pl.BlockSpec((1, LANES), lambda b, g, i: (0, 0)),
            pl.BlockSpec((1, LANES), lambda b, g, i: (0, 0)),
        ],
        out_specs=pl.BlockSpec((tq, width), lambda b, g, i: (b * nq + i, g)),
        out_shape=jax.ShapeDtypeStruct((t, FOX_WIDTH), BF16),
        scratch_shapes=[pltpu.VMEM((2 * FOX_PAIRS_PER_STEP, s, LANES), BF16)] + 2 * FOX_PAIRS_PER_STEP * [
            pltpu.VMEM((tq, LANES), F32),
            pltpu.VMEM((tq, LANES), F32),
            pltpu.VMEM((tq, LANES), F32),
        ],
        compiler_params=_params("parallel", "parallel", "arbitrary"),
        name="fox_attention",
    )(u, u, u, c, c, gq, gk)


def kernel(x, mem, norm_ffn1, ffn1_w_gate, ffn1_w_up, ffn1_w_down, norm_mix, norm_ffn2, ffn2_w_gate, ffn2_w_up, ffn2_w_down, norm_mem, mem_w_kv, mem_q_norm, mem_k_norm, conv_w_in, conv_w, conv_w_out, mla_w_in, mla_q_a_norm, mla_w_q_b, mla_kv_a_norm, mla_w_kv_b, mla_q_norm, mla_k_norm, mla_w_out, swa_w_in, swa_q_norm, swa_k_norm, swa_sinks, swa_w_out, rel_bias, fox_w_in, fox_b_f, fox_q_norm, fox_k_norm, fox_w_out):
    b, s, d = x.shape
    m_len = mem.shape[1]
    depth = norm_ffn1.shape[0]
    xt = x.reshape(b * s, d)
    mem2 = mem.reshape(b * m_len, d)
    bf = lambda w: w.astype(BF16)
    ffn1_w = (bf(ffn1_w_gate), bf(ffn1_w_up), bf(ffn1_w_down))
    ffn2_w = (bf(ffn2_w_gate), bf(ffn2_w_up), bf(ffn2_w_down))

    for i in range(depth):
        kind, occ = i % 4, i // 4
        xt = _ffn(xt, norm_ffn1, *ffn1_w, i)

        if kind == 0:
            c = conv_w.shape[-1]
            u = _norm_matmul(xt, norm_mix[i], bf(conv_w_in[occ]), name="conv_in_proj")
            memq_block = 3 * c // MEM_WIDTH
            mix = _conv_mixer(u, conv_w[occ], b)
            w_out = conv_w_out[occ]
        elif kind == 1:
            w_in = bf(mla_w_in[occ])
            lat = MLA_Q_RANK + MLA_KV_RANK
            w_in = jnp.concatenate([w_in[:, :lat], w_in[:, lat + MLA_ROPE:], w_in[:, lat:lat + MLA_ROPE],
                                    jnp.zeros((d, LANES - MLA_ROPE), BF16)], axis=1)
            u = _norm_matmul(xt, norm_mix[i], w_in, name="mla_in_proj")
            memq_block = lat // MEM_WIDTH
            krope_block = (lat + MEM_WIDTH) // LANES
            wq = mla_w_q_b[occ].reshape(MLA_Q_RANK, MLA_HEADS, MLA_QK)
            wq = jnp.concatenate([wq, jnp.zeros((MLA_Q_RANK, MLA_HEADS, MLA_QPAD - MLA_QK), F32)], axis=-1)
            wkv = mla_w_kv_b[occ].reshape(MLA_KV_RANK, MLA_HEADS, MLA_NOPE + MLA_V)
            wkv = jnp.concatenate([wkv[..., :MLA_NOPE].reshape(MLA_KV_RANK, -1),
                                   wkv[..., MLA_NOPE:].reshape(MLA_KV_RANK, -1)], axis=1)
            q = _norm_matmul(u, mla_q_a_norm[occ], bf(wq.reshape(MLA_Q_RANK, -1)), col_block=0, name="mla_q_proj")
            kv = _norm_matmul(u, mla_kv_a_norm[occ], bf(wkv), col_block=1, name="mla_kv_proj")
            pad = jnp.zeros((MLA_QPAD - MLA_QK,), F32)
            gq = jnp.concatenate([mla_q_norm[occ], pad]).reshape(1, MLA_QPAD)
            gk = jnp.concatenate([mla_k_norm[occ], pad]).reshape(1, MLA_QPAD)
            cos_t, sin_t = _rope_tables(s)
            mix = _mla_attention(q, kv, u, krope_block, cos_t, sin_t, gq, gk, b)
            w_out = mla_w_out[occ]
        elif kind == 2:
            u = _norm_matmul(xt, norm_mix[i], bf(swa_w_in[occ]), name="swa_in_proj")
            memq_block = (SWA_Q_HEADS + 2 * SWA_KV_HEADS) * SWA_HEAD_DIM // MEM_WIDTH
            mix = _swa_attention(u, swa_q_norm[occ], swa_k_norm[occ], swa_sinks[occ], rel_bias, b)
            w_out = swa_w_out[occ]
        else:
            w_in = bf(fox_w_in[occ])
            qkv = 3 * FOX_WIDTH
            w_main = jnp.concatenate([w_in[:, :qkv], w_in[:, qkv + FOX_HEADS:]], axis=1)
            w_gate = jnp.concatenate([w_in[:, qkv:qkv + FOX_HEADS], jnp.zeros((d, LANES - FOX_HEADS), BF16)], axis=1)
            u = _norm_matmul(xt, norm_mix[i], w_main, name="fox_in_proj")
            u_f = _norm_matmul(xt, norm_mix[i], w_gate, out_dtype=F32, name="fox_gate_proj")
            b_f = jnp.concatenate([fox_b_f[occ], jnp.zeros((LANES - FOX_HEADS,), F32)]).reshape(1, LANES)
            c_gate = _fox_gate_cumsum(u_f, b_f, b)
            memq_block = qkv // MEM_WIDTH
            mix = _fox_attention(u, c_gate, fox_q_norm[occ], fox_k_norm[occ], b)
            w_out = fox_w_out[occ]

        mkv = _norm_matmul(mem2, norm_mem[i], bf(mem_w_kv[i]), name="mem_kv_proj")
        mem_out = _mem_attn(u, memq_block, mkv, mem_q_norm[i], mem_k_norm[i], b)
        k_mix = mix.shape[1]
        xt = _out_proj(mix, mem_out, bf(w_out[:k_mix]), bf(w_out[k_mix:]), xt)

        xt = _ffn(xt, norm_ffn2, *ffn2_w, i)
    return xt.reshape(b, s, d)
```

```python
import functools
import math

import numpy as np
import jax
import jax.numpy as jnp
from jax import lax
from jax.experimental import pallas as pl
from jax.experimental.pallas import tpu as pltpu

F32 = jnp.float32
BF16 = jnp.bfloat16

RMS_EPS = 1e-6
NEG_BIG = -1e30
LOG2_E = math.log2(math.e)
LANES = 128
HALF_LANES = LANES // 2
VMEM_LIMIT_BYTES = 56 * 1024 * 1024

BLOCK = 128
MEM_HEADS = 4
MEM_HEAD_DIM = 128
MEM_WIDTH = MEM_HEADS * MEM_HEAD_DIM
CONV_K = 3
MLA_HEADS = 16
MLA_Q_RANK = 512
MLA_KV_RANK = 512
MLA_NOPE = 128
MLA_ROPE = 64
MLA_V = 128
MLA_QK = MLA_NOPE + MLA_ROPE
MLA_QPAD = 2 * LANES
ROPE_THETA = 10000.0
SWA_Q_HEADS = 32
SWA_KV_HEADS = 4
SWA_GROUP = SWA_Q_HEADS // SWA_KV_HEADS
SWA_HEAD_DIM = 64
WINDOW = 128
REL_BUCKETS = 32
REL_MAX_DIST = 128
FOX_HEADS = 32
FOX_HEAD_DIM = 64
FOX_WIDTH = FOX_HEADS * FOX_HEAD_DIM


def _params(*semantics):
    return pltpu.CompilerParams(dimension_semantics=semantics, vmem_limit_bytes=VMEM_LIMIT_BYTES)


def _pick_tile(n, target):
    best = None
    for t in range(LANES, min(n, target) + 1, LANES):
        if n % t == 0:
            best = t
    return n if best is None else best


def _rms_rows(x, gain):
    r = lax.rsqrt(jnp.mean(x * x, axis=-1, keepdims=True) + RMS_EPS)
    return x * r * gain


def _rms_half_lanes(x, gain):
    lane = lax.broadcasted_iota(jnp.int32, x.shape, 1)
    low = lane < HALF_LANES
    sq = x * x
    s_low = jnp.sum(jnp.where(low, sq, 0.0), axis=-1, keepdims=True)
    s_high = jnp.sum(jnp.where(low, 0.0, sq), axis=-1, keepdims=True)
    r = lax.rsqrt(jnp.where(low, s_low, s_high) * (1.0 / HALF_LANES) + RMS_EPS)
    return x * r * gain


def _dot(a, b):
    return jnp.dot(a, b, preferred_element_type=F32)


def _dot_nt(a, b):
    return lax.dot_general(a, b, (((1,), (1,)), ((), ())), preferred_element_type=F32)


FFN_ROWS = 1024
FFN_SPLIT = 2


def _ffn_kernel(x_ref, g_ref, wg_ref, wu_ref, wd_ref, o_ref, h_scr):
    @pl.when(pl.program_id(1) == 0)
    def _():
        x = x_ref[...]
        h_scr[...] = _rms_rows(x, g_ref[...]).astype(BF16)
        o_ref[...] = x

    h = h_scr[...]
    width = wg_ref.shape[-1] // FFN_SPLIT
    acts = []
    for c in range(FFN_SPLIT):
        cols = slice(c * width, (c + 1) * width)
        gate = _dot(h, wg_ref[:, cols])
        up = _dot(h, wu_ref[:, cols])
        acts.append((gate / (1.0 + jnp.exp(-gate)) * (0.5 * up)).astype(BF16))
    o_ref[...] += _dot(jnp.concatenate(acts, axis=-1), wd_ref[...])


def _ffn(x, gain, w_gate, w_up, w_down, layer):
    t, d = x.shape
    f = w_gate.shape[-1]
    tm = _pick_tile(t, FFN_ROWS)
    tf = _pick_tile(f, 512)
    return pl.pallas_call(
        _ffn_kernel,
        grid=(t // tm, f // tf),
        in_specs=[
            pl.BlockSpec((tm, d), lambda i, j: (i, 0)),
            pl.BlockSpec((None, 1, d), lambda i, j: (layer, 0, 0)),
            pl.BlockSpec((None, d, tf), lambda i, j: (layer, 0, j)),
            pl.BlockSpec((None, d, tf), lambda i, j: (layer, 0, j)),
            pl.BlockSpec((None, tf, d), lambda i, j: (layer, j, 0)),
        ],
        out_specs=pl.BlockSpec((tm, d), lambda i, j: (i, 0)),
        out_shape=jax.ShapeDtypeStruct((t, d), F32),
        scratch_shapes=[pltpu.VMEM((tm, d), BF16)],
        compiler_params=_params("parallel", "arbitrary"),
        name="ffn",
    )(x, gain.reshape(gain.shape[0], 1, d), w_gate, w_up, w_down)


NORM_MATMUL_COLS_LONG_K = 1792
NORM_MATMUL_COLS_SHORT_K = 2048


def _norm_matmul_kernel(x_ref, g_ref, w_ref, o_ref, h_scr):
    @pl.when(pl.program_id(1) == 0)
    def _():
        h_scr[...] = _rms_rows(x_ref[...].astype(F32), g_ref[...]).astype(BF16)

    o_ref[...] = _dot(h_scr[...], w_ref[...]).astype(o_ref.dtype)


def _norm_matmul(x, gain, w, *, col_block=0, out_dtype=BF16, name="norm_matmul"):
    m = x.shape[0]
    k, n = w.shape
    tm = _pick_tile(m, 1024)
    tn = n if n <= 2048 else _pick_tile(n, NORM_MATMUL_COLS_LONG_K if k > 1024 else NORM_MATMUL_COLS_SHORT_K)
    return pl.pallas_call(
        _norm_matmul_kernel,
        grid=(m // tm, n // tn),
        in_specs=[
            pl.BlockSpec((tm, k), lambda i, j: (i, col_block)),
            pl.BlockSpec((1, k), lambda i, j: (0, 0)),
            pl.BlockSpec((k, tn), lambda i, j: (0, j)),
        ],
        out_specs=pl.BlockSpec((tm, tn), lambda i, j: (i, j)),
        out_shape=jax.ShapeDtypeStruct((m, n), out_dtype),
        scratch_shapes=[pltpu.VMEM((tm, k), BF16)],
        compiler_params=_params("parallel", "arbitrary"),
        name=name,
    )(x, gain.reshape(1, k), w)


def _mem_attn_heads(q_ref, kv_ref, gq_ref, gk_ref):
    scale = 1.0 / math.sqrt(MEM_HEAD_DIM)
    outs = []
    for h in range(MEM_HEADS):
        cols = slice(h * MEM_HEAD_DIM, (h + 1) * MEM_HEAD_DIM)
        vcols = slice(MEM_WIDTH + h * MEM_HEAD_DIM, MEM_WIDTH + (h + 1) * MEM_HEAD_DIM)
        q = (_rms_rows(q_ref[:, cols].astype(F32), gq_ref[...]) * scale).astype(BF16)
        k = _rms_rows(kv_ref[:, cols].astype(F32), gk_ref[...]).astype(BF16)
        s = _dot_nt(q, k)
        p = jnp.exp(s - jnp.max(s, axis=-1, keepdims=True))
        l = jnp.sum(p, axis=-1, keepdims=True)
        outs.append((_dot(p.astype(BF16), kv_ref[:, vcols]) / l).astype(BF16))
    return jnp.concatenate(outs, axis=-1)


def _out_proj_kernel(a_ref, q_ref, kv_ref, gq_ref, gk_ref, wa_ref, wb_ref, x_ref, o_ref):
    mem_out = _mem_attn_heads(q_ref, kv_ref, gq_ref, gk_ref)
    acc = _dot(a_ref[...], wa_ref[...])
    acc = acc + _dot(mem_out, wb_ref[...])
    o_ref[...] = x_ref[...] + acc


def _out_proj(mix, u, q_col_block, mkv, gq, gk, w_mix, w_mem, x, batch):
    t, d = x.shape
    ka = mix.shape[1]
    m_len = mkv.shape[0] // batch
    tm = _pick_tile(t // batch, 512)
    tiles_per_batch = t // batch // tm
    tn = _pick_tile(d, 2048)
    return pl.pallas_call(
        _out_proj_kernel,
        grid=(t // tm, d // tn),
        in_specs=[
            pl.BlockSpec((tm, ka), lambda i, j: (i, 0)),
            pl.BlockSpec((tm, MEM_WIDTH), lambda i, j: (i, q_col_block)),
            pl.BlockSpec((m_len, 2 * MEM_WIDTH), lambda i, j: (i // tiles_per_batch, 0)),
            pl.BlockSpec((1, MEM_HEAD_DIM), lambda i, j: (0, 0)),
            pl.BlockSpec((1, MEM_HEAD_DIM), lambda i, j: (0, 0)),
            pl.BlockSpec((ka, tn), lambda i, j: (0, j)),
            pl.BlockSpec((MEM_WIDTH, tn), lambda i, j: (0, j)),
            pl.BlockSpec((tm, tn), lambda i, j: (i, j)),
        ],
        out_specs=pl.BlockSpec((tm, tn), lambda i, j: (i, j)),
        out_shape=jax.ShapeDtypeStruct((t, d), F32),
        compiler_params=_params("parallel", "parallel"),
        name="out_proj",
    )(mix, u, mkv, gq.reshape(1, MEM_HEAD_DIM), gk.reshape(1, MEM_HEAD_DIM), w_mix, w_mem, x)


CONV_HALO = 16


def _conv_kernel(gb_ref, gc_ref, xt_ref, gch_ref, xth_ref, w_ref, o_ref, *, tiles_per_seq):
    i = pl.program_id(0)
    z = gc_ref[...].astype(F32) * xt_ref[...].astype(F32)
    zh = gch_ref[...].astype(F32) * xth_ref[...].astype(F32)
    zh = jnp.where(i % tiles_per_seq == 0, 0.0, zh)
    zc = jnp.concatenate([zh[CONV_HALO - 8:], z], axis=0)
    z1 = pltpu.roll(zc, 1, 0)[8:]
    z2 = pltpu.roll(zc, 2, 0)[8:]
    w = w_ref[...]
    conv = z2 * w[0:1] + z1 * w[1:2] + z * w[2:3]
    o_ref[...] = (gb_ref[...].astype(F32) * conv).astype(o_ref.dtype)


def _conv_mixer(u, conv_w, batch):
    t = u.shape[0]
    s = t // batch
    c = conv_w.shape[1]
    ts = _pick_tile(s, 512)
    tc = _pick_tile(c, 512)
    nc = c // tc
    halo_per_tile = ts // CONV_HALO

    def halo_map(col0):
        return lambda i, j: (jnp.maximum(i * halo_per_tile - 1, 0), col0 + j)

    return pl.pallas_call(
        functools.partial(_conv_kernel, tiles_per_seq=s // ts),
        grid=(t // ts, nc),
        in_specs=[
            pl.BlockSpec((ts, tc), lambda i, j: (i, j)),
            pl.BlockSpec((ts, tc), lambda i, j: (i, nc + j)),
            pl.BlockSpec((ts, tc), lambda i, j: (i, 2 * nc + j)),
            pl.BlockSpec((CONV_HALO, tc), halo_map(nc)),
            pl.BlockSpec((CONV_HALO, tc), halo_map(2 * nc)),
            pl.BlockSpec((CONV_K, tc), lambda i, j: (0, j)),
        ],
        out_specs=pl.BlockSpec((ts, tc), lambda i, j: (i, j)),
        out_shape=jax.ShapeDtypeStruct((t, c), BF16),
        compiler_params=_params("parallel", "parallel"),
        name="conv_mixer",
    )(u, u, u, u, u, conv_w)


def _online_softmax_step(s, v, m_ref, l_ref, acc_ref):
    m_prev = m_ref[...]
    m_new = jnp.maximum(m_prev, jnp.max(s, axis=-1, keepdims=True))
    alpha = jnp.exp2(m_prev - m_new)
    p = [jnp.exp2(s[:, t * LANES:(t + 1) * LANES] - m_new) for t in range(s.shape[1] // LANES)]
    l_ref[...] = alpha * l_ref[...] + functools.reduce(lambda a, b: a + b, p)
    acc_ref[...] = alpha * acc_ref[...] + _dot(jnp.concatenate(p, axis=-1).astype(BF16), v)
    m_ref[...] = m_new


def _softmax_finish(l_ref, acc_ref):
    return acc_ref[...] / jnp.sum(l_ref[...], axis=-1, keepdims=True)


def _causal_mask(s):
    row = lax.broadcasted_iota(jnp.int32, s.shape, 0)
    col = lax.broadcasted_iota(jnp.int32, s.shape, 1)
    return jnp.where(col <= row, s, NEG_BIG)


def _swap_rope_halves(t):
    lane = lax.broadcasted_iota(jnp.int32, t.shape, 1)
    up = pltpu.roll(t, MLA_ROPE // 2, 1)
    down = pltpu.roll(t, LANES - MLA_ROPE // 2, 1)
    return jnp.where(lane < MLA_ROPE // 2, down, up)


def _rope_lanes(t, cos_t, sin_t):
    return t * cos_t + _swap_rope_halves(t) * sin_t


def _rms_rope_part(x, gain):
    r = lax.rsqrt(jnp.sum(x * x, axis=-1, keepdims=True) * (1.0 / MLA_ROPE) + RMS_EPS)
    return x * r * gain


MLA_HEADS_PER_STEP = 8


def _mla_attn_kernel(q_ref, kn_ref, v_ref, kr_ref, cosq_ref, sinq_ref, cosk_ref, sink_ref,
                     gq_ref, gk_ref, o_ref, k_scr, *state_scr, tq):
    i = pl.program_id(2)
    state = [state_scr[3 * e:3 * e + 3] for e in range(MLA_HEADS_PER_STEP)]

    @pl.when(i == 0)
    def _():
        kr = _rms_rope_part(kr_ref[...].astype(F32), gk_ref[:, LANES:])
        kr = _rope_lanes(kr, cosk_ref[...], sink_ref[...]).astype(BF16)
        for e in range(MLA_HEADS_PER_STEP):
            kn = kn_ref[:, e * LANES:(e + 1) * LANES].astype(F32)
            k_scr[e, :, :LANES] = _rms_rows(kn, gk_ref[:, :LANES]).astype(BF16)
            k_scr[e, :, LANES:] = kr

    scale = LOG2_E / math.sqrt(MLA_QK)
    q_heads = []
    for e in range(MLA_HEADS_PER_STEP):
        c0 = e * MLA_QPAD
        q_nope = _rms_rows(q_ref[:, c0:c0 + LANES].astype(F32), gq_ref[:, :LANES])
        q_rot = _rms_rope_part(q_ref[:, c0 + LANES:c0 + MLA_QPAD].astype(F32), gq_ref[:, LANES:])
        q_rot = _rope_lanes(q_rot, cosq_ref[...], sinq_ref[...])
        q_heads.append((jnp.concatenate([q_nope, q_rot], axis=-1) * scale).astype(BF16))

    for m_scr, l_scr, acc_scr in state:
        m_scr[...] = jnp.full(m_scr.shape, NEG_BIG, F32)
        l_scr[...] = jnp.zeros(l_scr.shape, F32)
        acc_scr[...] = jnp.zeros(acc_scr.shape, F32)

    def chunk(j, masked):
        rows = pl.ds(pl.multiple_of(j * tq, tq), tq)
        for e in range(MLA_HEADS_PER_STEP):
            s = _dot_nt(q_heads[e], k_scr[e, rows, :])
            if masked:
                s = _causal_mask(s)
            _online_softmax_step(s, v_ref[rows, e * MLA_V:(e + 1) * MLA_V], *state[e])

    def body(j, carry):
        chunk(j, False)
        return carry

    lax.fori_loop(0, i, body, 0)
    chunk(i, True)
    for e, (_, l_scr, acc_scr) in enumerate(state):
        o_ref[:, e * MLA_V:(e + 1) * MLA_V] = _softmax_finish(l_scr, acc_scr).astype(o_ref.dtype)


def _mla_attention(q, kv, u, krope_block, cos_t, sin_t, gq, gk, batch):
    t = q.shape[0]
    s = t // batch
    tq = 512 if s % 512 == 0 and s > 512 else BLOCK
    nq = s // tq
    h = MLA_HEADS
    hs = MLA_HEADS_PER_STEP
    groups = h // hs
    kernel = functools.partial(_mla_attn_kernel, tq=tq)
    return pl.pallas_call(
        kernel,
        grid=(batch, groups, nq),
        in_specs=[
            pl.BlockSpec((tq, hs * MLA_QPAD), lambda b, hh, i: (b * nq + i, hh)),
            pl.BlockSpec((s, hs * LANES), lambda b, hh, i: (b, hh)),
            pl.BlockSpec((s, hs * LANES), lambda b, hh, i: (b, groups + hh)),
            pl.BlockSpec((s, LANES), lambda b, hh, i: (b, krope_block)),
            pl.BlockSpec((tq, LANES), lambda b, hh, i: (i, 0)),
            pl.BlockSpec((tq, LANES), lambda b, hh, i: (i, 0)),
            pl.BlockSpec((s, LANES), lambda b, hh, i: (0, 0)),
            pl.BlockSpec((s, LANES), lambda b, hh, i: (0, 0)),
            pl.BlockSpec((1, MLA_QPAD), lambda b, hh, i: (0, 0)),
            pl.BlockSpec((1, MLA_QPAD), lambda b, hh, i: (0, 0)),
        ],
        out_specs=pl.BlockSpec((tq, hs * MLA_V), lambda b, hh, i: (b * nq + i, hh)),
        out_shape=jax.ShapeDtypeStruct((t, h * MLA_V), BF16),
        scratch_shapes=[pltpu.VMEM((hs, s, MLA_QPAD), BF16)] + hs * [
            pltpu.VMEM((tq, LANES), F32),
            pltpu.VMEM((tq, LANES), F32),
            pltpu.VMEM((tq, MLA_V), F32),
        ],
        compiler_params=_params("parallel", "parallel", "arbitrary"),
        name="mla_attention",
    )(q, kv, kv, u, cos_t, sin_t, cos_t, sin_t, gq, gk)


def _rope_tables(s):
    half = MLA_ROPE // 2
    inv = ROPE_THETA ** (-jnp.arange(half, dtype=F32) / half)
    ang = jnp.arange(s).astype(F32)[:, None] * inv
    cos, sin = jnp.cos(ang), jnp.sin(ang)
    zeros = jnp.zeros((s, LANES - MLA_ROPE), F32)
    return (jnp.concatenate([cos, cos, zeros], axis=-1), jnp.concatenate([-sin, sin, zeros], axis=-1))


SWA_CHUNKS = SWA_GROUP * SWA_HEAD_DIM // LANES


def _swa_kernel(sinks_ref, q_ref, kp_ref, kc_ref, vp_ref, vc_ref, bias_ref, gq_ref, gk_ref, o_ref):
    n = pl.program_id(1)
    lane = lax.broadcasted_iota(jnp.int32, (BLOCK, LANES), 1)
    scale = 1.0 / math.sqrt(SWA_HEAD_DIM)
    group_cols = SWA_CHUNKS * LANES
    row_chunk = lax.broadcasted_iota(jnp.int32, (SWA_CHUNKS * BLOCK, 1), 0) // BLOCK

    for kh in range(SWA_KV_HEADS):
        pair, half = divmod(kh, 2)
        pair_cols = slice(pair * LANES, (pair + 1) * LANES)
        if half == 0:
            kp_n = _rms_half_lanes(kp_ref[:, pair_cols].astype(F32), gk_ref[...])
            kc_n = _rms_half_lanes(kc_ref[:, pair_cols].astype(F32), gk_ref[...])
            vp_f = vp_ref[:, pair_cols].astype(F32)
            vc_f = vc_ref[:, pair_cols].astype(F32)
        own = lane // HALF_LANES == half
        col0 = kh * group_cols

        def by_parity(x):
            mine = jnp.where(own, x, 0.0)
            other = pltpu.roll(mine, HALF_LANES, 1)
            low, high = (mine, other) if half == 0 else (other, mine)
            return low.astype(BF16), high.astype(BF16)

        kp, kc, vp, vc = by_parity(kp_n), by_parity(kc_n), by_parity(vp_f), by_parity(vc_f)
        q = jnp.concatenate(
            [_rms_half_lanes(q_ref[:, col0 + c * LANES:col0 + (c + 1) * LANES].astype(F32), gq_ref[...]) * scale
             for c in range(SWA_CHUNKS)], axis=0).astype(BF16)

        out = None
        for e in range(2):
            sink = jnp.zeros((SWA_CHUNKS * BLOCK, 1), F32)
            for c in range(SWA_CHUNKS):
                sink = jnp.where(row_chunk == c, sinks_ref[kh * SWA_GROUP + 2 * c + e], sink)
            s_prev = jnp.where(n > 0, _dot_nt(q, kp[e]) + bias_ref[kh, e, 0], NEG_BIG)
            s_cur = _dot_nt(q, kc[e]) + bias_ref[kh, e, 1]
            m = jnp.maximum(jnp.max(jnp.maximum(s_prev, s_cur), axis=-1, keepdims=True), sink)
            p_prev = jnp.exp(s_prev - m)
            p_cur = jnp.exp(s_cur - m)
            l = jnp.sum(p_prev + p_cur, axis=-1, keepdims=True) + jnp.exp(sink - m)
            o = (_dot(p_prev.astype(BF16), vp[e]) + _dot(p_cur.astype(BF16), vc[e])) / l
            out = o if out is None else out + o
        for c in range(SWA_CHUNKS):
            o_ref[:, col0 + c * LANES:col0 + (c + 1) * LANES] = out[c * BLOCK:(c + 1) * BLOCK].astype(o_ref.dtype)


def _t5_causal_bucket(dist):
    exact = REL_BUCKETS // 2
    d = np.maximum(dist, 0)
    log_b = exact + (np.log(np.maximum(d, 1) / exact) / np.log(REL_MAX_DIST / exact)
                     * (REL_BUCKETS - exact)).astype(np.int32)
    log_b = np.minimum(log_b, REL_BUCKETS - 1)
    return np.where(d < exact, d, log_b).astype(np.int32)


def _swa_bias_table(rel_bias):
    hq = rel_bias.shape[1]
    by_dist = rel_bias.astype(F32)[_t5_causal_bucket(np.arange(WINDOW))]
    masked = lambda width: jnp.full((hq, width), NEG_BIG, F32)
    n = 3 * BLOCK - 1
    w = jnp.concatenate([masked(BLOCK), by_dist[::-1].T, masked(BLOCK - 1)], axis=1)
    u = jnp.roll(w, -(BLOCK - 1), axis=1)
    flat = jnp.broadcast_to(u[:, None, :], (hq, BLOCK, n)).reshape(hq, BLOCK * n)
    bias = flat[:, :BLOCK * (n - 1)].reshape(hq, BLOCK, n - 1)[:, :, :2 * BLOCK]
    bias = bias.reshape(SWA_KV_HEADS, SWA_CHUNKS, 2, BLOCK, 2, BLOCK)
    bias = bias.transpose(0, 2, 4, 1, 3, 5)
    return bias.reshape(SWA_KV_HEADS, 2, 2, SWA_CHUNKS * BLOCK, BLOCK)


def _swa_attention(u, q_norm, k_norm, sinks, rel_bias, batch):
    t = u.shape[0]
    s = t // batch
    nblk = s // BLOCK
    nq_cols = SWA_Q_HEADS * SWA_HEAD_DIM
    gq = jnp.tile(q_norm.reshape(1, SWA_HEAD_DIM), (1, 2))
    gk = jnp.tile(k_norm.reshape(1, SWA_HEAD_DIM), (1, 2))

    kv_cols = SWA_KV_HEADS * SWA_HEAD_DIM

    def prev_map(col0):
        return lambda b, n: (b * nblk + jnp.maximum(n - 1, 0), col0 // kv_cols)

    def cur_map(col0):
        return lambda b, n: (b * nblk + n, col0 // kv_cols)

    return pl.pallas_call(
        _swa_kernel,
        grid=(batch, nblk),
        in_specs=[
            pl.BlockSpec(memory_space=pltpu.SMEM),
            pl.BlockSpec((BLOCK, nq_cols), lambda b, n: (b * nblk + n, 0)),
            pl.BlockSpec((BLOCK, kv_cols), prev_map(nq_cols)),
            pl.BlockSpec((BLOCK, kv_cols), cur_map(nq_cols)),
            pl.BlockSpec((BLOCK, kv_cols), prev_map(nq_cols + kv_cols)),
            pl.BlockSpec((BLOCK, kv_cols), cur_map(nq_cols + kv_cols)),
            pl.BlockSpec((SWA_KV_HEADS, 2, 2, SWA_CHUNKS * BLOCK, BLOCK), lambda b, n: (0, 0, 0, 0, 0)),
            pl.BlockSpec((1, LANES), lambda b, n: (0, 0)),
            pl.BlockSpec((1, LANES), lambda b, n: (0, 0)),
        ],
        out_specs=pl.BlockSpec((BLOCK, nq_cols), lambda b, n: (b * nblk + n, 0)),
        out_shape=jax.ShapeDtypeStruct((t, nq_cols), BF16),
        compiler_params=_params("parallel", "parallel"),
        name="swa_attention",
    )(sinks.astype(F32), u, u, u, u, u, _swa_bias_table(rel_bias), gq, gk)


def _fox_gate_kernel(u_ref, b_ref, o_ref):
    x = u_ref[...] + b_ref[...]
    log_f = -(jnp.maximum(-x, 0.0) + jnp.log1p(jnp.exp(-jnp.abs(x))))
    row = lax.broadcasted_iota(jnp.int32, log_f.shape, 0)
    c = log_f
    shift = 1
    while shift < c.shape[0]:
        c = c + jnp.where(row >= shift, pltpu.roll(c, shift, 0), 0.0)
        shift *= 2
    o_ref[...] = c


def _fox_gate_cumsum(u_f, b_f, batch):
    t = u_f.shape[0]
    s = t // batch
    return pl.pallas_call(
        _fox_gate_kernel,
        grid=(batch,),
        in_specs=[pl.BlockSpec((s, LANES), lambda b: (b, 0)), pl.BlockSpec((1, LANES), lambda b: (0, 0))],
        out_specs=pl.BlockSpec((s, LANES), lambda b: (b, 0)),
        out_shape=jax.ShapeDtypeStruct((t, LANES), F32),
        compiler_params=_params("parallel"),
        name="fox_gate_cumsum",
    )(u_f, b_f)


FOX_GATE_PIECES = 3


def _split_bf16_pieces(x):
    pieces = []
    for _ in range(FOX_GATE_PIECES - 1):
        piece = x.astype(BF16).astype(F32)
        pieces.append(piece)
        x = x - piece
    return pieces + [x]


def _fox_operand(x, gate_col, lane, head, is_query):
    base = (1 - head) * HALF_LANES
    out = jnp.where(lane // HALF_LANES == head, x, 0.0)
    for t, piece in enumerate(_split_bf16_pieces(gate_col)):
        gate_lane, one_lane = base + t, base + FOX_GATE_PIECES + t
        if not is_query:
            gate_lane, one_lane, piece = one_lane, gate_lane, -piece
        out = jnp.where(lane == gate_lane, piece, out)
        out = jnp.where(lane == one_lane, 1.0, out)
    return out.astype(BF16)


FOX_PAIRS_PER_STEP = 4


def _fox_attn_kernel(q_ref, k_ref, v_ref, cq_ref, ck_ref, gq_ref, gk_ref, o_ref, k_scr, *state_scr, tq):
    group = pl.program_id(1)
    i = pl.program_id(2)
    heads = [(p, e) for p in range(FOX_PAIRS_PER_STEP) for e in range(2)]
    state = [state_scr[3 * n:3 * n + 3] for n in range(len(heads))]

    def gate_col(c, p, e):
        lane = lax.broadcasted_iota(jnp.int32, c.shape, 1)
        head = 2 * (FOX_PAIRS_PER_STEP * group + p) + e
        return jnp.sum(jnp.where(lane == head, c, 0.0), axis=-1, keepdims=True) * LOG2_E

    def pair_cols(p):
        return slice(p * LANES, (p + 1) * LANES)

    @pl.when(i == 0)
    def _():
        for p in range(FOX_PAIRS_PER_STEP):
            kn = _rms_half_lanes(k_ref[:, pair_cols(p)].astype(F32), gk_ref[...])
            lane = lax.broadcasted_iota(jnp.int32, kn.shape, 1)
            for e in range(2):
                k_scr[2 * p + e] = _fox_operand(kn, gate_col(ck_ref[...], p, e), lane, e, False)

    scale = LOG2_E / math.sqrt(FOX_HEAD_DIM)
    lane = lax.broadcasted_iota(jnp.int32, (tq, LANES), 1)
    q_heads = []
    for p in range(FOX_PAIRS_PER_STEP):
        qn = _rms_half_lanes(q_ref[:, pair_cols(p)].astype(F32), gq_ref[...]) * scale
        q_heads += [_fox_operand(qn, gate_col(cq_ref[...], p, e), lane, e, True) for e in range(2)]

    for m_scr, l_scr, acc_scr in state:
        m_scr[...] = jnp.full(m_scr.shape, NEG_BIG, F32)
        l_scr[...] = jnp.zeros(l_scr.shape, F32)
        acc_scr[...] = jnp.zeros(acc_scr.shape, F32)

    def chunk(j, masked):
        rows = pl.ds(pl.multiple_of(j * tq, tq), tq)
        for n, (p, _) in enumerate(heads):
            s = _dot_nt(q_heads[n], k_scr[n, rows, :])
            if masked:
                s = _causal_mask(s)
            _online_softmax_step(s, v_ref[rows, pair_cols(p)], *state[n])

    def body(j, carry):
        chunk(j, False)
        return carry

    lax.fori_loop(0, i, body, 0)
    chunk(i, True)
    for p in range(FOX_PAIRS_PER_STEP):
        low, high = (_softmax_finish(*state[2 * p + e][1:]) for e in range(2))
        o_ref[:, pair_cols(p)] = jnp.where(lane < HALF_LANES, low, high).astype(o_ref.dtype)


def _fox_attention(u, c, q_norm, k_norm, batch):
    t = u.shape[0]
    s = t // batch
    tq = 512 if s % 512 == 0 and s > 512 else BLOCK
    nq = s // tq
    width = FOX_PAIRS_PER_STEP * LANES
    groups = FOX_WIDTH // width
    gq = jnp.tile(q_norm.reshape(1, FOX_HEAD_DIM), (1, 2))
    gk = jnp.tile(k_norm.reshape(1, FOX_HEAD_DIM), (1, 2))
    kernel = functools.partial(_fox_attn_kernel, tq=tq)
    return pl.pallas_call(
        kernel,
        grid=(batch, groups, nq),
        in_specs=[
            pl.BlockSpec((tq, width), lambda b, g, i: (b * nq + i, g)),
            pl.BlockSpec((s, width), lambda b, g, i: (b, groups + g)),
            pl.BlockSpec((s, width), lambda b, g, i: (b, 2 * groups + g)),
            pl.BlockSpec((tq, LANES), lambda b, g, i: (b * nq + i, 0)),
            pl.BlockSpec((s, LANES), lambda b, g, i: (b, 0)),
            pl.BlockSpec((1, LANES), lambda b, g, i: (0, 0)),
            pl.BlockSpec((1, LANES), lambda b, g, i: (0, 0)),
        ],
        out_specs=pl.BlockSpec((tq, width), lambda b, g, i: (b * nq + i, g)),
        out_shape=jax.ShapeDtypeStruct((t, FOX_WIDTH), BF16),
        scratch_shapes=[pltpu.VMEM((2 * FOX_PAIRS_PER_STEP, s, LANES), BF16)] + 2 * FOX_PAIRS_PER_STEP * [
            pltpu.VMEM((tq, LANES), F32),
            pltpu.VMEM((tq, LANES), F32),
            pltpu.VMEM((tq, LANES), F32),
        ],
        compiler_params=_params("parallel", "parallel", "arbitrary"),
        name="fox_attention",
    )(u, u, u, c, c, gq, gk)


def kernel(x, mem, norm_ffn1, ffn1_w_gate, ffn1_w_up, ffn1_w_down, norm_mix, norm_ffn2, ffn2_w_gate, ffn2_w_up, ffn2_w_down, norm_mem, mem_w_kv, mem_q_norm, mem_k_norm, conv_w_in, conv_w, conv_w_out, mla_w_in, mla_q_a_norm, mla_w_q_b, mla_kv_a_norm, mla_w_kv_b, mla_q_norm, mla_k_norm, mla_w_out, swa_w_in, swa_q_norm, swa_k_norm, swa_sinks, swa_w_out, rel_bias, fox_w_in, fox_b_f, fox_q_norm, fox_k_norm, fox_w_out):
    b, s, d = x.shape
    m_len = mem.shape[1]
    depth = norm_ffn1.shape[0]
    xt = x.reshape(b * s, d)
    mem2 = mem.reshape(b * m_len, d)
    bf = lambda w: w.astype(BF16)
    ffn1_w = (bf(ffn1_w_gate), bf(ffn1_w_up), bf(ffn1_w_down))
    ffn2_w = (bf(ffn2_w_gate), bf(ffn2_w_up), bf(ffn2_w_down))

    for i in range(depth):
        kind, occ = i % 4, i // 4
        xt = _ffn(xt, norm_ffn1, *ffn1_w, i)

        if kind == 0:
            c = conv_w.shape[-1]
            u = _norm_matmul(xt, norm_mix[i], bf(conv_w_in[occ]), name="conv_in_proj")
            memq_block = 3 * c // MEM_WIDTH
            mix = _conv_mixer(u, conv_w[occ], b)
            w_out = conv_w_out[occ]
        elif kind == 1:
            w_in = bf(mla_w_in[occ])
            lat = MLA_Q_RANK + MLA_KV_RANK
            w_in = jnp.concatenate([w_in[:, :lat], w_in[:, lat + MLA_ROPE:], w_in[:, lat:lat + MLA_ROPE],
                                    jnp.zeros((d, LANES - MLA_ROPE), BF16)], axis=1)
            u = _norm_matmul(xt, norm_mix[i], w_in, name="mla_in_proj")
            memq_block = lat // MEM_WIDTH
            krope_block = (lat + MEM_WIDTH) // LANES
            wq = mla_w_q_b[occ].reshape(MLA_Q_RANK, MLA_HEADS, MLA_QK)
            wq = jnp.concatenate([wq, jnp.zeros((MLA_Q_RANK, MLA_HEADS, MLA_QPAD - MLA_QK), F32)], axis=-1)
            wkv = mla_w_kv_b[occ].reshape(MLA_KV_RANK, MLA_HEADS, MLA_NOPE + MLA_V)
            wkv = jnp.concatenate([wkv[..., :MLA_NOPE].reshape(MLA_KV_RANK, -1),
                                   wkv[..., MLA_NOPE:].reshape(MLA_KV_RANK, -1)], axis=1)
            q = _norm_matmul(u, mla_q_a_norm[occ], bf(wq.reshape(MLA_Q_RANK, -1)), col_block=0, name="mla_q_proj")
            kv = _norm_matmul(u, mla_kv_a_norm[occ], bf(wkv), col_block=1, name="mla_kv_proj")
            pad = jnp.zeros((MLA_QPAD - MLA_QK,), F32)
            gq = jnp.concatenate([mla_q_norm[occ], pad]).reshape(1, MLA_QPAD)
            gk = jnp.concatenate([mla_k_norm[occ], pad]).reshape(1, MLA_QPAD)
            cos_t, sin_t = _rope_tables(s)
            mix = _mla_attention(q, kv, u, krope_block, cos_t, sin_t, gq, gk, b)
            w_out = mla_w_out[occ]
        elif kind == 2:
            u = _norm_matmul(xt, norm_mix[i], bf(swa_w_in[occ]), name="swa_in_proj")
            memq_block = (SWA_Q_HEADS + 2 * SWA_KV_HEADS) * SWA_HEAD_DIM // MEM_WIDTH
            mix = _swa_attention(u, swa_q_norm[occ], swa_k_norm[occ], swa_sinks[occ], rel_bias, b)
            w_out = swa_w_out[occ]
        else:
            w_in = bf(fox_w_in[occ])
            qkv = 3 * FOX_WIDTH
            w_main = jnp.concatenate([w_in[:, :qkv], w_in[:, qkv + FOX_HEADS:]], axis=1)
            w_gate = jnp.concatenate([w_in[:, qkv:qkv + FOX_HEADS], jnp.zeros((d, LANES - FOX_HEADS), BF16)], axis=1)
            u = _norm_matmul(xt, norm_mix[i], w_main, name="fox_in_proj")
            u_f = _norm_matmul(xt, norm_mix[i], w_gate, out_dtype=F32, name="fox_gate_proj")
            b_f = jnp.concatenate([fox_b_f[occ], jnp.zeros((LANES - FOX_HEADS,), F32)]).reshape(1, LANES)
            c_gate = _fox_gate_cumsum(u_f, b_f, b)
            memq_block = qkv // MEM_WIDTH
            mix = _fox_attention(u, c_gate, fox_q_norm[occ], fox_k_norm[occ], b)
            w_out = fox_w_out[occ]

        mkv = _norm_matmul(mem2, norm_mem[i], bf(mem_w_kv[i]), name="mem_kv_proj")
        k_mix = mix.shape[1]
        xt = _out_proj(mix, u, memq_block, mkv, mem_q_norm[i], mem_k_norm[i],
                       bf(w_out[:k_mix]), bf(w_out[k_mix:]), xt, b)

        xt = _ffn(xt, norm_ffn2, *ffn2_w, i)
    return xt.reshape(b, s, d)
```
